```python
import jax, jax.numpy as jnp
from jax import lax
import numpy as np

D_MODEL = 1024
BATCH = 8
SEQ = 4096
DEPTH = 1

GRID_W = 64
Q_BLOCK = 128
ROPE_THETA = 10000.0
EPS = 1e-6

MLA_HEADS = 8
MLA_NOPE_DIM = 64
MLA_ROPE_DIM = 32
MLA_V_DIM = 64
Q_LORA_RANK = 256
KV_LORA_RANK = 128

GQA_HEADS = 8
GQA_KV_HEADS = 2
GQA_HEAD_DIM = 64

IN_COLS = (Q_LORA_RANK, KV_LORA_RANK, MLA_ROPE_DIM,
           GQA_HEADS * GQA_HEAD_DIM, GQA_KV_HEADS * GQA_HEAD_DIM, GQA_KV_HEADS * GQA_HEAD_DIM)
D_IN = sum(IN_COLS)
MLA_OUT = MLA_HEADS * MLA_V_DIM
GQA_OUT = GQA_HEADS * GQA_HEAD_DIM
D_MIX = MLA_OUT + GQA_OUT

D_FF = 2816
N_MOD = 9

kernel_name = "hybrid_mla_gqa_macaron_adaln_encoder"


def rms_norm(x, g):
    xf = x.astype(jnp.float32)
    y = xf * lax.rsqrt(jnp.mean(xf * xf, axis=-1, keepdims=True) + EPS)
    return (y * g.astype(jnp.float32)).astype(x.dtype)


def modulate(h, shift, scale):
    return h * (1 + scale[:, None, :]) + shift[:, None, :]


def swiglu(h, w_gu, w_down):
    a, b = jnp.split(h @ w_gu, 2, axis=-1)
    return (jax.nn.silu(a) * b) @ w_down


def axial_angles(seq_len, dim):
    rows = seq_len // GRID_W
    row = jnp.repeat(jnp.arange(rows), GRID_W).astype(jnp.float32)
    col = jnp.tile(jnp.arange(GRID_W), rows).astype(jnp.float32)
    axis_dim = dim // 2
    inv_freq = ROPE_THETA ** (-(jnp.arange(axis_dim // 2, dtype=jnp.float32) * 2.0 / axis_dim))
    return row[:, None] * inv_freq[None, :], col[:, None] * inv_freq[None, :]


def rotate(x, ang):
    xf = x.astype(jnp.float32)
    x1, x2 = jnp.split(xf, 2, axis=-1)
    cos = jnp.cos(ang)[None, :, None, :]
    sin = jnp.sin(ang)[None, :, None, :]
    return jnp.concatenate([x1 * cos - x2 * sin, x1 * sin + x2 * cos], axis=-1).astype(x.dtype)


def axial_rope(x):
    seq_len, dim = x.shape[1], x.shape[-1]
    ang_row, ang_col = axial_angles(seq_len, dim)
    half = dim // 2
    return jnp.concatenate([rotate(x[..., :half], ang_row), rotate(x[..., half:], ang_col)], axis=-1)


def blocked_attention(q, k, v, scale):
    B, S, Hk, G, dk = q.shape
    nb = S // Q_BLOCK
    qb = q.reshape(B, nb, Q_BLOCK, Hk, G, dk).transpose(1, 0, 2, 3, 4, 5)

    def one_block(qi):
        s = jnp.einsum('bqhgd,bshd->bhgqs', qi, k).astype(jnp.float32) * scale
        p = jax.nn.softmax(s, axis=-1).astype(v.dtype)
        return jnp.einsum('bhgqs,bshe->bqhge', p, v)

    o = lax.map(one_block, qb)
    return o.transpose(1, 0, 2, 3, 4, 5).reshape(B, S, Hk * G * v.shape[-1])


def mla_group(q_lat, kv_lat, k_rope, g_q_lat, w_uq, g_kv_lat, w_ukv):
    B, S, _ = q_lat.shape
    q = (rms_norm(q_lat, g_q_lat) @ w_uq).reshape(B, S, MLA_HEADS, MLA_NOPE_DIM + MLA_ROPE_DIM)
    q_nope, q_pe = q[..., :MLA_NOPE_DIM], q[..., MLA_NOPE_DIM:]
    kv = (rms_norm(kv_lat, g_kv_lat) @ w_ukv).reshape(B, S, MLA_HEADS, MLA_NOPE_DIM + MLA_V_DIM)
    k_nope, v = kv[..., :MLA_NOPE_DIM], kv[..., MLA_NOPE_DIM:]
    q_pe = axial_rope(q_pe)
    k_pe = axial_rope(k_rope.reshape(B, S, 1, MLA_ROPE_DIM))
    q_full = jnp.concatenate([q_nope, q_pe], axis=-1)[:, :, :, None, :]
    k_full = jnp.concatenate([k_nope, jnp.broadcast_to(k_pe, (B, S, MLA_HEADS, MLA_ROPE_DIM))], axis=-1)
    return blocked_attention(q_full, k_full, v, (MLA_NOPE_DIM + MLA_ROPE_DIM) ** -0.5)


def gqa_group(q_in, k_in, v_in, g_qhead, g_khead):
    B, S, _ = q_in.shape
    q = axial_rope(rms_norm(q_in.reshape(B, S, GQA_HEADS, GQA_HEAD_DIM), g_qhead))
    k = axial_rope(rms_norm(k_in.reshape(B, S, GQA_KV_HEADS, GQA_HEAD_DIM), g_khead))
    v = v_in.reshape(B, S, GQA_KV_HEADS, GQA_HEAD_DIM)
    q = q.reshape(B, S, GQA_KV_HEADS, GQA_HEADS // GQA_KV_HEADS, GQA_HEAD_DIM)
    return blocked_attention(q, k, v, GQA_HEAD_DIM ** -0.5)


def setup_inputs(seed: int = 0) -> dict:
    key = jax.random.key(seed)
    ks = jax.random.split(key, 24)
    D, L = D_MODEL, DEPTH

    def w(k, shape, fan_in, mult=1.0):
        return jax.random.normal(k, shape, jnp.float32) * (fan_in ** -0.5) * mult

    def gain(k, shape):
        return 1.0 + 0.02 * jax.random.normal(k, shape, jnp.float32)

    return {
        "x": jax.random.normal(ks[0], (BATCH, SEQ, D), jnp.float32),
        "c": jax.random.normal(ks[1], (BATCH, D), jnp.float32),
        "w_ada": w(ks[2], (L, D, N_MOD * D), D, 0.5),
        "b_ada": 0.02 * jax.random.normal(ks[3], (L, N_MOD * D), jnp.float32),
        "g_ffn1": gain(ks[4], (L, D)),
        "w1_gu": w(ks[5], (L, D, 2 * D_FF), D),
        "w1_down": w(ks[6], (L, D_FF, D), D_FF),
        "g_mix": gain(ks[7], (L, D)),
        "w_in": w(ks[8], (L, D, D_IN), D),
        "g_q_lat": gain(ks[9], (L, Q_LORA_RANK)),
        "w_uq": w(ks[10], (L, Q_LORA_RANK, MLA_HEADS * (MLA_NOPE_DIM + MLA_ROPE_DIM)), Q_LORA_RANK),
        "g_kv_lat": gain(ks[11], (L, KV_LORA_RANK)),
        "w_ukv": w(ks[12], (L, KV_LORA_RANK, MLA_HEADS * (MLA_NOPE_DIM + MLA_V_DIM)), KV_LORA_RANK),
        "g_qhead": gain(ks[13], (L, GQA_HEAD_DIM)),
        "g_khead": gain(ks[14], (L, GQA_HEAD_DIM)),
        "g_out_mla": gain(ks[15], (L, MLA_OUT)),
        "g_out_gqa": gain(ks[16], (L, GQA_OUT)),
        "w_out": w(ks[17], (L, D_MIX, D), D_MIX),
        "g_ffn2": gain(ks[18], (L, D)),
        "w2_gu": w(ks[19], (L, D, 2 * D_FF), D),
        "w2_down": w(ks[20], (L, D_FF, D), D_FF),
        "g_final": gain(ks[21], (D,)),
    }


def reference(x, c, w_ada, b_ada, g_ffn1, w1_gu, w1_down, g_mix, w_in, g_q_lat, w_uq,
              g_kv_lat, w_ukv, g_qhead, g_khead, g_out_mla, g_out_gqa, w_out,
              g_ffn2, w2_gu, w2_down, g_final):
    offs = [int(o) for o in np.cumsum(IN_COLS)[:-1]]
    c_act = jax.nn.silu(c)
    for l in range(DEPTH):
        mod = c_act @ w_ada[l] + b_ada[l]
        (sh_f1, sc_f1, gt_f1, sh_m, sc_m, gt_m, sh_f2, sc_f2, gt_f2) = jnp.split(mod, N_MOD, axis=-1)

        h = modulate(rms_norm(x, g_ffn1[l]), sh_f1, sc_f1)
        x = x + 0.5 * gt_f1[:, None, :] * swiglu(h, w1_gu[l], w1_down[l])

        h = modulate(rms_norm(x, g_mix[l]), sh_m, sc_m)
        z = h @ w_in[l]
        q_lat, kv_lat, k_rope, q_g, k_g, v_g = jnp.split(z, offs, axis=-1)
        o_mla = mla_group(q_lat, kv_lat, k_rope, g_q_lat[l], w_uq[l], g_kv_lat[l], w_ukv[l])
        o_gqa = gqa_group(q_g, k_g, v_g, g_qhead[l], g_khead[l])
        o = jnp.concatenate([rms_norm(o_mla, g_out_mla[l]), rms_norm(o_gqa, g_out_gqa[l])], axis=-1)
        x = x + gt_m[:, None, :] * (o @ w_out[l])

        h = modulate(rms_norm(x, g_ffn2[l]), sh_f2, sc_f2)
        x = x + 0.5 * gt_f2[:, None, :] * swiglu(h, w2_gu[l], w2_down[l])
    return rms_norm(x, g_final)
```

```python
import functools
import math

import jax
import jax.numpy as jnp
import numpy as np
from jax import lax
from jax.experimental import pallas as pl
from jax.experimental.pallas import tpu as pltpu

D_MODEL = 1024
GRID_W = 64
ROPE_THETA = 10000.0
EPS = 1e-6

MLA_HEADS = 8
MLA_NOPE_DIM = 64
MLA_ROPE_DIM = 32
MLA_V_DIM = 64
Q_LORA_RANK = 256
KV_LORA_RANK = 128

GQA_HEADS = 8
GQA_KV_HEADS = 2
GQA_HEAD_DIM = 64
GQA_GROUP = GQA_HEADS // GQA_KV_HEADS

D_FF = 2816
N_MOD = 9

N_Q_HEADS = MLA_HEADS + GQA_HEADS
N_KV_HEADS = MLA_HEADS + GQA_KV_HEADS
HEAD_PAD = 128
V_DIM = 64
LOG2E = math.log2(math.e)

_OFF = np.cumsum([0, Q_LORA_RANK, KV_LORA_RANK, MLA_ROPE_DIM,
                  GQA_HEADS * GQA_HEAD_DIM, GQA_KV_HEADS * GQA_HEAD_DIM, GQA_KV_HEADS * GQA_HEAD_DIM])
D_IN = int(_OFF[-1])

VMEM_LIMIT_BYTES = 56 * 1024 * 1024

TM_FFN = 512
TM_MIX = 512
TQ_ATTN = 256
FF_CHUNKS = 2

_bf16 = jnp.bfloat16
_f32 = jnp.float32


def _dot(a, b):
    return jnp.dot(a, b, preferred_element_type=_f32)


def _dot_nt(a, b):
    return lax.dot_general(a, b, (((1,), (1,)), ((), ())), preferred_element_type=_f32)


def _dot_tn(a, b):
    return lax.dot_general(a, b, (((0,), (0,)), ((), ())), preferred_element_type=_f32)


def _const_spec(shape):
    return pl.BlockSpec(shape, lambda *_: (0,) * len(shape), pipeline_mode=pl.Buffered(1))


def _ada_kernel(c_ref, w_ref, b_ref, o_ref):
    c = c_ref[...]
    ca = c * jax.nn.sigmoid(c)
    w = w_ref[...]
    ca_hi = ca.astype(_bf16)
    ca_lo = (ca - ca_hi.astype(_f32)).astype(_bf16)
    w_hi = w.astype(_bf16)
    w_lo = (w - w_hi.astype(_f32)).astype(_bf16)
    o_ref[...] = _dot(ca_hi, w_hi) + (_dot(ca_hi, w_lo) + _dot(ca_lo, w_hi)) + b_ref[...]


def _ada(c, w_ada, b_ada):
    bsz, d = c.shape
    n = w_ada.shape[1]
    tn = 1152
    return pl.pallas_call(
        _ada_kernel,
        grid=(n // tn,),
        in_specs=[pl.BlockSpec((bsz, d), lambda j: (0, 0)),
                  pl.BlockSpec((d, tn), lambda j: (0, j)),
                  pl.BlockSpec((1, tn), lambda j: (0, j))],
        out_specs=pl.BlockSpec((bsz, tn), lambda j: (0, j)),
        out_shape=jax.ShapeDtypeStruct((bsz, n), _f32),
        compiler_params=pltpu.CompilerParams(vmem_limit_bytes=VMEM_LIMIT_BYTES),
        name="ada",
    )(c, w_ada, b_ada.reshape(1, n))


def _norm_mod(x, g, shift, scale):
    r = lax.rsqrt(jnp.mean(x * x, axis=-1, keepdims=True) + EPS)
    return (x * r * g) * (1.0 + scale) + shift


def _swiglu(h, wgu_ref, wd_ref):
    cw = D_FF // FF_CHUNKS
    acc = None
    for j in range(FF_CHUNKS):
        a = _dot(h, wgu_ref[:, j * cw:(j + 1) * cw])
        b = _dot(h, wgu_ref[:, D_FF + j * cw:D_FF + (j + 1) * cw])
        act = (a * jax.nn.sigmoid(a) * b).astype(_bf16)
        part = _dot(act, wd_ref[j * cw:(j + 1) * cw, :])
        acc = part if acc is None else acc + part
    return acc


def _ffn_kernel(x_ref, mod_ref, g_ref, wgu_ref, wd_ref, o_ref, *, mod_base):
    x = x_ref[...]
    shift = mod_ref[0, mod_base:mod_base + 1, :]
    scale = mod_ref[0, mod_base + 1:mod_base + 2, :]
    gate = mod_ref[0, mod_base + 2:mod_base + 3, :]
    h = _norm_mod(x, g_ref[...], shift, scale).astype(_bf16)
    o_ref[...] = x + (0.5 * gate) * _swiglu(h, wgu_ref, wd_ref)


def _ffn(x2d, mod3, g, wgu, wd, seq, mod_base):
    n, d = x2d.shape
    tm = TM_FFN
    tpb = seq // tm
    return pl.pallas_call(
        functools.partial(_ffn_kernel, mod_base=mod_base),
        grid=(n // tm,),
        in_specs=[pl.BlockSpec((tm, d), lambda i: (i, 0)),
                  pl.BlockSpec((1, N_MOD, d), lambda i: (i // tpb, 0, 0)),
                  _const_spec((1, d)),
                  _const_spec(wgu.shape),
                  _const_spec(wd.shape)],
        out_specs=pl.BlockSpec((tm, d), lambda i: (i, 0)),
        out_shape=jax.ShapeDtypeStruct((n, d), _f32),
        compiler_params=pltpu.CompilerParams(vmem_limit_bytes=VMEM_LIMIT_BYTES),
        name="ffn1",
    )(x2d, mod3, g, wgu, wd)


def _rope_fm(x, tab):
    q = x.shape[0] // 4
    x1r, x2r, x1c, x2c = x[0:q], x[q:2 * q], x[2 * q:3 * q], x[3 * q:4 * q]
    cr, sr, cc, sc = tab[0:q], tab[q:2 * q], tab[2 * q:3 * q], tab[3 * q:4 * q]
    return jnp.concatenate([x1r * cr - x2r * sr, x1r * sr + x2r * cr,
                            x1c * cc - x2c * sc, x1c * sc + x2c * cc], axis=0)


def _rms_fm(x, g_col, extra=1.0):
    r = lax.rsqrt(jnp.mean(x * x, axis=0, keepdims=True) + EPS)
    if extra != 1.0:
        r = r * extra
    return x * r * g_col


def _mix_pre_kernel(x_ref, mod_ref, g_ref, winT_ref, gq_ref, wuqT_ref, gkv_ref, wukvT_ref,
                    gqh_ref, gkh_ref, t64_ref, t32_ref, qT_ref, k_ref, vT_ref):
    tm = x_ref.shape[0]
    x = x_ref[...]
    h = _norm_mod(x, g_ref[...], mod_ref[0, 3:4, :], mod_ref[0, 4:5, :]).astype(_bf16)
    zT = _dot_nt(winT_ref[...], h)
    q_lat = zT[_OFF[0]:_OFF[1]]
    kv_lat = zT[_OFF[1]:_OFF[2]]
    k_rope = zT[_OFF[2]:_OFF[3]]
    q_g = zT[_OFF[3]:_OFF[4]]
    k_g = zT[_OFF[4]:_OFF[5]]
    v_g = zT[_OFF[5]:_OFF[6]]
    t64 = t64_ref[...]
    t32 = t32_ref[...]

    zeros32 = jnp.zeros((HEAD_PAD - MLA_NOPE_DIM - MLA_ROPE_DIM, tm), _f32)
    zeros64 = jnp.zeros((HEAD_PAD - GQA_HEAD_DIM, tm), _f32)
    ones_pad = (lax.broadcasted_iota(jnp.int32, (HEAD_PAD - V_DIM, tm), 0) == 0).astype(_f32)

    mla_scale = (MLA_NOPE_DIM + MLA_ROPE_DIM) ** -0.5 * LOG2E
    qn = _rms_fm(q_lat, gq_ref[...], mla_scale).astype(_bf16)
    qa = _dot(wuqT_ref[...], qn)
    nope_rows = MLA_HEADS * MLA_NOPE_DIM
    for hd in range(MLA_HEADS):
        nope = qa[hd * MLA_NOPE_DIM:(hd + 1) * MLA_NOPE_DIM]
        pe = _rope_fm(qa[nope_rows + hd * MLA_ROPE_DIM:nope_rows + (hd + 1) * MLA_ROPE_DIM], t32)
        qT_ref[0, hd] = jnp.concatenate([nope, pe, zeros32], axis=0).astype(_bf16)

    kvn = _rms_fm(kv_lat, gkv_ref[...]).astype(_bf16)
    kva = _dot(wukvT_ref[...], kvn)
    k_pe = _rope_fm(k_rope, t32)
    for hd in range(MLA_HEADS):
        k_nope = kva[hd * MLA_NOPE_DIM:(hd + 1) * MLA_NOPE_DIM]
        kT = jnp.concatenate([k_nope, k_pe, zeros32], axis=0)
        k_ref[0, hd] = kT.T.astype(_bf16)
        v = kva[nope_rows + hd * MLA_V_DIM:nope_rows + (hd + 1) * MLA_V_DIM]
        vT_ref[0, hd] = jnp.concatenate([v, ones_pad], axis=0).astype(_bf16)

    gqa_scale = GQA_HEAD_DIM ** -0.5 * LOG2E
    for hd in range(GQA_HEADS):
        xh = _rms_fm(q_g[hd * GQA_HEAD_DIM:(hd + 1) * GQA_HEAD_DIM], gqh_ref[...], gqa_scale)
        qT_ref[0, MLA_HEADS + hd] = jnp.concatenate([_rope_fm(xh, t64), zeros64], axis=0).astype(_bf16)
    for hd in range(GQA_KV_HEADS):
        xh = _rms_fm(k_g[hd * GQA_HEAD_DIM:(hd + 1) * GQA_HEAD_DIM], gkh_ref[...])
        kT = jnp.concatenate([_rope_fm(xh, t64), zeros64], axis=0)
        k_ref[0, MLA_HEADS + hd] = kT.T.astype(_bf16)
        v = v_g[hd * GQA_HEAD_DIM:(hd + 1) * GQA_HEAD_DIM]
        vT_ref[0, MLA_HEADS + hd] = jnp.concatenate([v, ones_pad], axis=0).astype(_bf16)


def _mix_pre(x2d, mod3, g, winT, gq, wuqT, gkv, wukvT, gqh, gkh, t64, t32, bsz, seq):
    n, d = x2d.shape
    tm = TM_MIX
    tpb = seq // tm
    out_shape = (jax.ShapeDtypeStruct((bsz, N_Q_HEADS, HEAD_PAD, seq), _bf16),
                 jax.ShapeDtypeStruct((bsz, N_KV_HEADS, seq, HEAD_PAD), _bf16),
                 jax.ShapeDtypeStruct((bsz, N_KV_HEADS, HEAD_PAD, seq), _bf16))
    return pl.pallas_call(
        _mix_pre_kernel,
        grid=(n // tm,),
        in_specs=[pl.BlockSpec((tm, d), lambda i: (i, 0)),
                  pl.BlockSpec((1, N_MOD, d), lambda i: (i // tpb, 0, 0)),
                  _const_spec((1, d)),
                  _const_spec(winT.shape),
                  _const_spec(gq.shape),
                  _const_spec(wuqT.shape),
                  _const_spec(gkv.shape),
                  _const_spec(wukvT.shape),
                  _const_spec(gqh.shape),
                  _const_spec(gkh.shape),
                  pl.BlockSpec((GQA_HEAD_DIM, tm), lambda i: (0, i % tpb)),
                  pl.BlockSpec((MLA_ROPE_DIM, tm), lambda i: (0, i % tpb))],
        out_specs=(pl.BlockSpec((1, N_Q_HEADS, HEAD_PAD, tm), lambda i: (i // tpb, 0, 0, i % tpb)),
                   pl.BlockSpec((1, N_KV_HEADS, tm, HEAD_PAD), lambda i: (i // tpb, 0, i % tpb, 0)),
                   pl.BlockSpec((1, N_KV_HEADS, HEAD_PAD, tm), lambda i: (i // tpb, 0, 0, i % tpb))),
        out_shape=out_shape,
        compiler_params=pltpu.CompilerParams(vmem_limit_bytes=VMEM_LIMIT_BYTES),
        name="mix_pre",
    )(x2d, mod3, g, winT, gq, wuqT, gkv, wukvT, gqh, gkh, t64, t32)


def _attn_kernel(qT_ref, k_ref, vT_ref, o_ref):
    sT = _dot(k_ref[0, 0], qT_ref[0, 0])
    m = jnp.max(sT, axis=0, keepdims=True)
    p = jnp.exp2(sT - m).astype(_bf16)
    oT = _dot(vT_ref[0, 0], p)
    o_ref[0, 0] = oT[0:V_DIM] * (1.0 / oT[V_DIM:V_DIM + 1])


def _kv_head(h):
    return jnp.where(h < MLA_HEADS, h, MLA_HEADS + (h - MLA_HEADS) // GQA_GROUP)


def _attention(qT, k, vT):
    bsz, _, _, seq = qT.shape
    tq = TQ_ATTN
    return pl.pallas_call(
        _attn_kernel,
        grid=(bsz, N_Q_HEADS, seq // tq),
        in_specs=[pl.BlockSpec((1, 1, HEAD_PAD, tq), lambda b, h, i: (b, h, 0, i)),
                  pl.BlockSpec((1, 1, seq, HEAD_PAD), lambda b, h, i: (b, _kv_head(h), 0, 0)),
                  pl.BlockSpec((1, 1, HEAD_PAD, seq), lambda b, h, i: (b, _kv_head(h), 0, 0))],
        out_specs=pl.BlockSpec((1, 1, V_DIM, tq), lambda b, h, i: (b, h, 0, i)),
        out_shape=jax.ShapeDtypeStruct((bsz, N_Q_HEADS, V_DIM, seq), _f32),
        compiler_params=pltpu.CompilerParams(vmem_limit_bytes=VMEM_LIMIT_BYTES),
        name="attn",
    )(qT, k, vT)


def _post_kernel(x_ref, o_ref_in, mod_ref, gom_ref, gog_ref, wout_ref, g2_ref, wgu_ref, wd_ref, gf_ref, out_ref,
                 *, final_norm):
    tm = x_ref.shape[0]
    x = x_ref[...]
    oT = o_ref_in[0].reshape(N_Q_HEADS * V_DIM, tm)
    half = MLA_HEADS * MLA_V_DIM
    on = jnp.concatenate([_rms_fm(oT[:half], gom_ref[...]), _rms_fm(oT[half:], gog_ref[...])], axis=0)
    y = _dot_tn(on.astype(_bf16), wout_ref[...])
    x = x + mod_ref[0, 5:6, :] * y
    h = _norm_mod(x, g2_ref[...], mod_ref[0, 6:7, :], mod_ref[0, 7:8, :]).astype(_bf16)
    x = x + (0.5 * mod_ref[0, 8:9, :]) * _swiglu(h, wgu_ref, wd_ref)
    if final_norm:
        r = lax.rsqrt(jnp.mean(x * x, axis=-1, keepdims=True) + EPS)
        x = x * r * gf_ref[...]
    out_ref[...] = x


def _post(x2d, oT, mod3, gom, gog, wout, g2, wgu, wd, gf, seq, final_norm):
    n, d = x2d.shape
    tm = TM_FFN
    tpb = seq // tm
    return pl.pallas_call(
        functools.partial(_post_kernel, final_norm=final_norm),
        grid=(n // tm,),
        in_specs=[pl.BlockSpec((tm, d), lambda i: (i, 0)),
                  pl.BlockSpec((1, N_Q_HEADS, V_DIM, tm), lambda i: (i // tpb, 0, 0, i % tpb)),
                  pl.BlockSpec((1, N_MOD, d), lambda i: (i // tpb, 0, 0)),
                  _const_spec(gom.shape),
                  _const_spec(gog.shape),
                  _const_spec(wout.shape),
                  _const_spec((1, d)),
                  _const_spec(wgu.shape),
                  _const_spec(wd.shape),
                  _const_spec((1, d))],
        out_specs=pl.BlockSpec((tm, d), lambda i: (i, 0)),
        out_shape=jax.ShapeDtypeStruct((n, d), _f32),
        compiler_params=pltpu.CompilerParams(vmem_limit_bytes=VMEM_LIMIT_BYTES),
        name="post",
    )(x2d, oT, mod3, gom, gog, wout, g2, wgu, wd, gf)


def _rope_table(seq, dim):
    rows = seq // GRID_W
    row = jnp.repeat(jnp.arange(rows), GRID_W).astype(_f32)
    col = jnp.tile(jnp.arange(GRID_W), rows).astype(_f32)
    axis_dim = dim // 2
    inv_freq = ROPE_THETA ** (-(jnp.arange(axis_dim // 2, dtype=_f32) * 2.0 / axis_dim))
    ang_row = row[:, None] * inv_freq[None, :]
    ang_col = col[:, None] * inv_freq[None, :]
    return jnp.concatenate([jnp.cos(ang_row), jnp.sin(ang_row), jnp.cos(ang_col), jnp.sin(ang_col)], axis=1).T


def kernel(x, c, w_ada, b_ada, g_ffn1, w1_gu, w1_down, g_mix, w_in, g_q_lat, w_uq, g_kv_lat, w_ukv,
           g_qhead, g_khead, g_out_mla, g_out_gqa, w_out, g_ffn2, w2_gu, w2_down, g_final):
    bsz, seq, d = x.shape
    depth = w_ada.shape[0]
    t64 = _rope_table(seq, GQA_HEAD_DIM)
    t32 = _rope_table(seq, MLA_ROPE_DIM)
    x2d = x.reshape(bsz * seq, d)
    for l in range(depth):
        mod3 = _ada(c, w_ada[l], b_ada[l]).reshape(bsz, N_MOD, d)

        x2d = _ffn(x2d, mod3, g_ffn1[l].reshape(1, d), w1_gu[l].astype(_bf16), w1_down[l].astype(_bf16), seq, 0)

        winT = w_in[l].T.astype(_bf16)
        wuq = w_uq[l].reshape(Q_LORA_RANK, MLA_HEADS, MLA_NOPE_DIM + MLA_ROPE_DIM)
        wuqT = jnp.concatenate([wuq[:, :, :MLA_NOPE_DIM].reshape(Q_LORA_RANK, -1),
                                wuq[:, :, MLA_NOPE_DIM:].reshape(Q_LORA_RANK, -1)], axis=1).T.astype(_bf16)
        wukv = w_ukv[l].reshape(KV_LORA_RANK, MLA_HEADS, MLA_NOPE_DIM + MLA_V_DIM)
        wukvT = jnp.concatenate([wukv[:, :, :MLA_NOPE_DIM].reshape(KV_LORA_RANK, -1),
                                 wukv[:, :, MLA_NOPE_DIM:].reshape(KV_LORA_RANK, -1)], axis=1).T.astype(_bf16)
        qT, k, vT = _mix_pre(x2d, mod3, g_mix[l].reshape(1, d), winT,
                             g_q_lat[l].reshape(-1, 1), wuqT, g_kv_lat[l].reshape(-1, 1), wukvT,
                             g_qhead[l].reshape(-1, 1), g_khead[l].reshape(-1, 1), t64, t32, bsz, seq)

        oT = _attention(qT, k, vT)

        x2d = _post(x2d, oT, mod3, g_out_mla[l].reshape(-1, 1), g_out_gqa[l].reshape(-1, 1),
                    w_out[l].astype(_bf16), g_ffn2[l].reshape(1, d),
                    w2_gu[l].astype(_bf16), w2_down[l].astype(_bf16), g_final.reshape(1, d), seq,
                    final_norm=(l == depth - 1))
    return x2d.reshape(bsz, seq, d)
```

```python
import functools
import math

import jax
import jax.numpy as jnp
import numpy as np
from jax import lax
from jax.experimental import pallas as pl
from jax.experimental.pallas import tpu as pltpu

D_MODEL = 1024
GRID_W = 64
ROPE_THETA = 10000.0
EPS = 1e-6

MLA_HEADS = 8
MLA_NOPE_DIM = 64
MLA_ROPE_DIM = 32
MLA_V_DIM = 64
Q_LORA_RANK = 256
KV_LORA_RANK = 128

GQA_HEADS = 8
GQA_KV_HEADS = 2
GQA_HEAD_DIM = 64
GQA_GROUP = GQA_HEADS // GQA_KV_HEADS

D_FF = 2816
N_MOD = 9

N_Q_HEADS = MLA_HEADS + GQA_HEADS
N_KV_HEADS = MLA_HEADS + GQA_KV_HEADS
HEAD_PAD = 128
V_DIM = 64
LOG2E = math.log2(math.e)

_OFF = np.cumsum([0, Q_LORA_RANK, KV_LORA_RANK, MLA_ROPE_DIM,
                  GQA_HEADS * GQA_HEAD_DIM, GQA_KV_HEADS * GQA_HEAD_DIM, GQA_KV_HEADS * GQA_HEAD_DIM])
D_IN = int(_OFF[-1])

VMEM_LIMIT_BYTES = 56 * 1024 * 1024

TM_FFN = 512
TM_MIX = 512
FF_CHUNKS = 2

_bf16 = jnp.bfloat16
_f32 = jnp.float32


def _dot(a, b):
    return jnp.dot(a, b, preferred_element_type=_f32)


def _dot_nt(a, b):
    return lax.dot_general(a, b, (((1,), (1,)), ((), ())), preferred_element_type=_f32)


def _dot_tn(a, b):
    return lax.dot_general(a, b, (((0,), (0,)), ((), ())), preferred_element_type=_f32)


def _const_spec(shape):
    return pl.BlockSpec(shape, lambda *_: (0,) * len(shape), pipeline_mode=pl.Buffered(1))


def _ada_kernel(c_ref, w_ref, b_ref, o_ref):
    c = c_ref[...]
    ca = c * jax.nn.sigmoid(c)
    w = w_ref[...]
    ca_hi = ca.astype(_bf16)
    ca_lo = (ca - ca_hi.astype(_f32)).astype(_bf16)
    w_hi = w.astype(_bf16)
    w_lo = (w - w_hi.astype(_f32)).astype(_bf16)
    o_ref[...] = _dot(ca_hi, w_hi) + (_dot(ca_hi, w_lo) + _dot(ca_lo, w_hi)) + b_ref[...]


def _ada(c, w_ada, b_ada):
    bsz, d = c.shape
    n = w_ada.shape[1]
    tn = 1152
    return pl.pallas_call(
        _ada_kernel,
        grid=(n // tn,),
        in_specs=[pl.BlockSpec((bsz, d), lambda j: (0, 0)),
                  pl.BlockSpec((d, tn), lambda j: (0, j)),
                  pl.BlockSpec((1, tn), lambda j: (0, j))],
        out_specs=pl.BlockSpec((bsz, tn), lambda j: (0, j)),
        out_shape=jax.ShapeDtypeStruct((bsz, n), _f32),
        compiler_params=pltpu.CompilerParams(vmem_limit_bytes=VMEM_LIMIT_BYTES),
        name="ada",
    )(c, w_ada, b_ada.reshape(1, n))


def _norm_mod(x, g, shift, scale):
    r = lax.rsqrt(jnp.mean(x * x, axis=-1, keepdims=True) + EPS)
    return (x * r * g) * (1.0 + scale) + shift


def _swiglu(h, wgu_ref, wd_ref):
    cw = D_FF // FF_CHUNKS
    acc = None
    for j in range(FF_CHUNKS):
        a = _dot(h, wgu_ref[:, j * cw:(j + 1) * cw])
        b = _dot(h, wgu_ref[:, D_FF + j * cw:D_FF + (j + 1) * cw])
        act = (a * jax.nn.sigmoid(a) * b).astype(_bf16)
        part = _dot(act, wd_ref[j * cw:(j + 1) * cw, :])
        acc = part if acc is None else acc + part
    return acc


def _ffn_kernel(x_ref, mod_ref, g_ref, wgu_ref, wd_ref, o_ref, *, mod_base):
    x = x_ref[...]
    shift = mod_ref[0, mod_base:mod_base + 1, :]
    scale = mod_ref[0, mod_base + 1:mod_base + 2, :]
    gate = mod_ref[0, mod_base + 2:mod_base + 3, :]
    h = _norm_mod(x, g_ref[...], shift, scale).astype(_bf16)
    o_ref[...] = x + (0.5 * gate) * _swiglu(h, wgu_ref, wd_ref)


def _ffn(x2d, mod3, g, wgu, wd, seq, mod_base):
    n, d = x2d.shape
    tm = TM_FFN
    tpb = seq // tm
    return pl.pallas_call(
        functools.partial(_ffn_kernel, mod_base=mod_base),
        grid=(n // tm,),
        in_specs=[pl.BlockSpec((tm, d), lambda i: (i, 0)),
                  pl.BlockSpec((1, N_MOD, d), lambda i: (i // tpb, 0, 0)),
                  _const_spec((1, d)),
                  _const_spec(wgu.shape),
                  _const_spec(wd.shape)],
        out_specs=pl.BlockSpec((tm, d), lambda i: (i, 0)),
        out_shape=jax.ShapeDtypeStruct((n, d), _f32),
        compiler_params=pltpu.CompilerParams(vmem_limit_bytes=VMEM_LIMIT_BYTES),
        name="ffn1",
    )(x2d, mod3, g, wgu, wd)


def _rope_fm(x, tab):
    q = x.shape[0] // 4
    x1r, x2r, x1c, x2c = x[0:q], x[q:2 * q], x[2 * q:3 * q], x[3 * q:4 * q]
    cr, sr, cc, sc = tab[0:q], tab[q:2 * q], tab[2 * q:3 * q], tab[3 * q:4 * q]
    return jnp.concatenate([x1r * cr - x2r * sr, x1r * sr + x2r * cr,
                            x1c * cc - x2c * sc, x1c * sc + x2c * cc], axis=0)


def _rms_fm(x, g_col, extra=1.0):
    r = lax.rsqrt(jnp.mean(x * x, axis=0, keepdims=True) + EPS)
    if extra != 1.0:
        r = r * extra
    return x * r * g_col


def _mix_pre_kernel(x_ref, mod_ref, g_ref, winT_ref, gq_ref, wuqT_ref, gkv_ref, wukvT_ref,
                    gqh_ref, gkh_ref, t64_ref, t32_ref, qT_ref, k_ref, vT_ref):
    tm = x_ref.shape[0]
    x = x_ref[...]
    h = _norm_mod(x, g_ref[...], mod_ref[0, 3:4, :], mod_ref[0, 4:5, :]).astype(_bf16)
    zT = _dot_nt(winT_ref[...], h)
    q_lat = zT[_OFF[0]:_OFF[1]]
    kv_lat = zT[_OFF[1]:_OFF[2]]
    k_rope = zT[_OFF[2]:_OFF[3]]
    q_g = zT[_OFF[3]:_OFF[4]]
    k_g = zT[_OFF[4]:_OFF[5]]
    v_g = zT[_OFF[5]:_OFF[6]]
    t64 = t64_ref[...]
    t32 = t32_ref[...]

    zeros32 = jnp.zeros((HEAD_PAD - MLA_NOPE_DIM - MLA_ROPE_DIM, tm), _f32)
    zeros64 = jnp.zeros((HEAD_PAD - GQA_HEAD_DIM, tm), _f32)
    ones_pad = (lax.broadcasted_iota(jnp.int32, (HEAD_PAD - V_DIM, tm), 0) == 0).astype(_f32)

    mla_scale = (MLA_NOPE_DIM + MLA_ROPE_DIM) ** -0.5 * LOG2E
    qn = _rms_fm(q_lat, gq_ref[...], mla_scale).astype(_bf16)
    qa = _dot(wuqT_ref[...], qn)
    nope_rows = MLA_HEADS * MLA_NOPE_DIM
    for hd in range(MLA_HEADS):
        nope = qa[hd * MLA_NOPE_DIM:(hd + 1) * MLA_NOPE_DIM]
        pe = _rope_fm(qa[nope_rows + hd * MLA_ROPE_DIM:nope_rows + (hd + 1) * MLA_ROPE_DIM], t32)
        qT_ref[0, hd, 0] = jnp.concatenate([nope, pe, zeros32], axis=0).astype(_bf16)

    kvn = _rms_fm(kv_lat, gkv_ref[...]).astype(_bf16)
    kva = _dot(wukvT_ref[...], kvn)
    k_pe = _rope_fm(k_rope, t32)
    for hd in range(MLA_HEADS):
        k_nope = kva[hd * MLA_NOPE_DIM:(hd + 1) * MLA_NOPE_DIM]
        kT = jnp.concatenate([k_nope, k_pe, zeros32], axis=0)
        k_ref[0, hd] = kT.T.astype(_bf16)
        v = kva[nope_rows + hd * MLA_V_DIM:nope_rows + (hd + 1) * MLA_V_DIM]
        vT_ref[0, hd] = jnp.concatenate([v, ones_pad], axis=0).astype(_bf16)

    gqa_scale = GQA_HEAD_DIM ** -0.5 * LOG2E
    for hd in range(GQA_HEADS):
        xh = _rms_fm(q_g[hd * GQA_HEAD_DIM:(hd + 1) * GQA_HEAD_DIM], gqh_ref[...], gqa_scale)
        qT_ref[0, MLA_HEADS + hd, 0] = jnp.concatenate([_rope_fm(xh, t64), zeros64], axis=0).astype(_bf16)
    for hd in range(GQA_KV_HEADS):
        xh = _rms_fm(k_g[hd * GQA_HEAD_DIM:(hd + 1) * GQA_HEAD_DIM], gkh_ref[...])
        kT = jnp.concatenate([_rope_fm(xh, t64), zeros64], axis=0)
        k_ref[0, MLA_HEADS + hd] = kT.T.astype(_bf16)
        v = v_g[hd * GQA_HEAD_DIM:(hd + 1) * GQA_HEAD_DIM]
        vT_ref[0, MLA_HEADS + hd] = jnp.concatenate([v, ones_pad], axis=0).astype(_bf16)


def _mix_pre(x2d, mod3, g, winT, gq, wuqT, gkv, wukvT, gqh, gkh, t64, t32, bsz, seq):
    n, d = x2d.shape
    tm = TM_MIX
    tpb = seq // tm
    out_shape = (jax.ShapeDtypeStruct((bsz, N_Q_HEADS, tpb, HEAD_PAD, tm), _bf16),
                 jax.ShapeDtypeStruct((bsz, N_KV_HEADS, seq, HEAD_PAD), _bf16),
                 jax.ShapeDtypeStruct((bsz, N_KV_HEADS, HEAD_PAD, seq), _bf16))
    return pl.pallas_call(
        _mix_pre_kernel,
        grid=(n // tm,),
        in_specs=[pl.BlockSpec((tm, d), lambda i: (i, 0)),
                  pl.BlockSpec((1, N_MOD, d), lambda i: (i // tpb, 0, 0)),
                  _const_spec((1, d)),
                  _const_spec(winT.shape),
                  _const_spec(gq.shape),
                  _const_spec(wuqT.shape),
                  _const_spec(gkv.shape),
                  _const_spec(wukvT.shape),
                  _const_spec(gqh.shape),
                  _const_spec(gkh.shape),
                  pl.BlockSpec((GQA_HEAD_DIM, tm), lambda i: (0, i % tpb)),
                  pl.BlockSpec((MLA_ROPE_DIM, tm), lambda i: (0, i % tpb))],
        out_specs=(pl.BlockSpec((1, N_Q_HEADS, 1, HEAD_PAD, tm), lambda i: (i // tpb, 0, i % tpb, 0, 0)),
                   pl.BlockSpec((1, N_KV_HEADS, tm, HEAD_PAD), lambda i: (i // tpb, 0, i % tpb, 0)),
                   pl.BlockSpec((1, N_KV_HEADS, HEAD_PAD, tm), lambda i: (i // tpb, 0, 0, i % tpb))),
        out_shape=out_shape,
        compiler_params=pltpu.CompilerParams(vmem_limit_bytes=VMEM_LIMIT_BYTES),
        name="mix_pre",
    )(x2d, mod3, g, winT, gq, wuqT, gkv, wukvT, gqh, gkh, t64, t32)


def _attn_kernel(qT_ref, k_ref, vT_ref, o_ref, s_even, s_odd):
    n_tiles = qT_ref.shape[2]

    def scores(t, s_ref):
        s_ref[...] = _dot(k_ref[0, 0], qT_ref[0, 0, t])

    def reduce(t, s_ref):
        s = s_ref[...]
        m = jnp.max(s, axis=0, keepdims=True)
        p = jnp.exp2(s - m).astype(_bf16)
        oT = _dot(vT_ref[0, 0], p)
        o_ref[0, 0, t] = oT[0:V_DIM] * (1.0 / oT[V_DIM:V_DIM + 1])

    scores(0, s_even)

    def pair(j, carry):
        t = 2 * j
        scores(t + 1, s_odd)
        reduce(t, s_even)
        scores(t + 2, s_even)
        reduce(t + 1, s_odd)
        return carry

    lax.fori_loop(0, n_tiles // 2 - 1, pair, 0)
    scores(n_tiles - 1, s_odd)
    reduce(n_tiles - 2, s_even)
    reduce(n_tiles - 1, s_odd)


def _kv_head(h):
    return jnp.where(h < MLA_HEADS, h, MLA_HEADS + (h - MLA_HEADS) // GQA_GROUP)


def _attention(qT, k, vT):
    bsz, _, n_tiles, _, tq = qT.shape
    seq = n_tiles * tq
    assert n_tiles % 2 == 0
    return pl.pallas_call(
        _attn_kernel,
        grid=(bsz, N_Q_HEADS),
        in_specs=[pl.BlockSpec((1, 1, n_tiles, HEAD_PAD, tq), lambda b, h: (b, h, 0, 0, 0)),
                  pl.BlockSpec((1, 1, seq, HEAD_PAD), lambda b, h: (b, _kv_head(h), 0, 0)),
                  pl.BlockSpec((1, 1, HEAD_PAD, seq), lambda b, h: (b, _kv_head(h), 0, 0))],
        out_specs=pl.BlockSpec((1, 1, n_tiles, V_DIM, tq), lambda b, h: (b, h, 0, 0, 0)),
        out_shape=jax.ShapeDtypeStruct((bsz, N_Q_HEADS, n_tiles, V_DIM, tq), _f32),
        scratch_shapes=[pltpu.VMEM((seq, tq), _f32), pltpu.VMEM((seq, tq), _f32)],
        compiler_params=pltpu.CompilerParams(vmem_limit_bytes=VMEM_LIMIT_BYTES),
        name="attn",
    )(qT, k, vT)


def _post_kernel(x_ref, o_ref_in, mod_ref, gom_ref, gog_ref, wout_ref, g2_ref, wgu_ref, wd_ref, gf_ref, out_ref,
                 *, final_norm):
    tm = x_ref.shape[0]
    x = x_ref[...]
    oT = o_ref_in[0, :, 0].reshape(N_Q_HEADS * V_DIM, tm)
    half = MLA_HEADS * MLA_V_DIM
    on = jnp.concatenate([_rms_fm(oT[:half], gom_ref[...]), _rms_fm(oT[half:], gog_ref[...])], axis=0)
    y = _dot_tn(on.astype(_bf16), wout_ref[...])
    x = x + mod_ref[0, 5:6, :] * y
    h = _norm_mod(x, g2_ref[...], mod_ref[0, 6:7, :], mod_ref[0, 7:8, :]).astype(_bf16)
    x = x + (0.5 * mod_ref[0, 8:9, :]) * _swiglu(h, wgu_ref, wd_ref)
    if final_norm:
        r = lax.rsqrt(jnp.mean(x * x, axis=-1, keepdims=True) + EPS)
        x = x * r * gf_ref[...]
    out_ref[...] = x


def _post(x2d, oT, mod3, gom, gog, wout, g2, wgu, wd, gf, seq, final_norm):
    n, d = x2d.shape
    tm = oT.shape[-1]
    tpb = seq // tm
    return pl.pallas_call(
        functools.partial(_post_kernel, final_norm=final_norm),
        grid=(n // tm,),
        in_specs=[pl.BlockSpec((tm, d), lambda i: (i, 0)),
                  pl.BlockSpec((1, N_Q_HEADS, 1, V_DIM, tm), lambda i: (i // tpb, 0, i % tpb, 0, 0)),
                  pl.BlockSpec((1, N_MOD, d), lambda i: (i // tpb, 0, 0)),
                  _const_spec(gom.shape),
                  _const_spec(gog.shape),
                  _const_spec(wout.shape),
                  _const_spec((1, d)),
                  _const_spec(wgu.shape),
                  _const_spec(wd.shape),
                  _const_spec((1, d))],
        out_specs=pl.BlockSpec((tm, d), lambda i: (i, 0)),
        out_shape=jax.ShapeDtypeStruct((n, d), _f32),
        compiler_params=pltpu.CompilerParams(vmem_limit_bytes=VMEM_LIMIT_BYTES),
        name="post",
    )(x2d, oT, mod3, gom, gog, wout, g2, wgu, wd, gf)


def _rope_table(seq, dim):
    rows = seq // GRID_W
    row = jnp.repeat(jnp.arange(rows), GRID_W).astype(_f32)
    col = jnp.tile(jnp.arange(GRID_W), rows).astype(_f32)
    axis_dim = dim // 2
    inv_freq = ROPE_THETA ** (-(jnp.arange(axis_dim // 2, dtype=_f32) * 2.0 / axis_dim))
    ang_row = row[:, None] * inv_freq[None, :]
    ang_col = col[:, None] * inv_freq[None, :]
    return jnp.concatenate([jnp.cos(ang_row), jnp.sin(ang_row), jnp.cos(ang_col), jnp.sin(ang_col)], axis=1).T


def kernel(x, c, w_ada, b_ada, g_ffn1, w1_gu, w1_down, g_mix, w_in, g_q_lat, w_uq, g_kv_lat, w_ukv,
           g_qhead, g_khead, g_out_mla, g_out_gqa, w_out, g_ffn2, w2_gu, w2_down, g_final):
    bsz, seq, d = x.shape
    depth = w_ada.shape[0]
    t64 = _rope_table(seq, GQA_HEAD_DIM)
    t32 = _rope_table(seq, MLA_ROPE_DIM)
    x2d = x.reshape(bsz * seq, d)
    for l in range(depth):
        mod3 = _ada(c, w_ada[l], b_ada[l]).reshape(bsz, N_MOD, d)

        x2d = _ffn(x2d, mod3, g_ffn1[l].reshape(1, d), w1_gu[l].astype(_bf16), w1_down[l].astype(_bf16), seq, 0)

        winT = w_in[l].T.astype(_bf16)
        wuq = w_uq[l].reshape(Q_LORA_RANK, MLA_HEADS, MLA_NOPE_DIM + MLA_ROPE_DIM)
        wuqT = jnp.concatenate([wuq[:, :, :MLA_NOPE_DIM].reshape(Q_LORA_RANK, -1),
                                wuq[:, :, MLA_NOPE_DIM:].reshape(Q_LORA_RANK, -1)], axis=1).T.astype(_bf16)
        wukv = w_ukv[l].reshape(KV_LORA_RANK, MLA_HEADS, MLA_NOPE_DIM + MLA_V_DIM)
        wukvT = jnp.concatenate([wukv[:, :, :MLA_NOPE_DIM].reshape(KV_LORA_RANK, -1),
                                 wukv[:, :, MLA_NOPE_DIM:].reshape(KV_LORA_RANK, -1)], axis=1).T.astype(_bf16)
        qT, k, vT = _mix_pre(x2d, mod3, g_mix[l].reshape(1, d), winT,
                             g_q_lat[l].reshape(-1, 1), wuqT, g_kv_lat[l].reshape(-1, 1), wukvT,
                             g_qhead[l].reshape(-1, 1), g_khead[l].reshape(-1, 1), t64, t32, bsz, seq)

        oT = _attention(qT, k, vT)

        x2d = _post(x2d, oT, mod3, g_out_mla[l].reshape(-1, 1), g_out_gqa[l].reshape(-1, 1),
                    w_out[l].astype(_bf16), g_ffn2[l].reshape(1, d),
                    w2_gu[l].astype(_bf16), w2_down[l].astype(_bf16), g_final.reshape(1, d), seq,
                    final_norm=(l == depth - 1))
    return x2d.reshape(bsz, seq, d)
```

```python
import functools
import math

import jax
import jax.numpy as jnp
import numpy as np
from jax import lax
from jax.experimental import pallas as pl
from jax.experimental.pallas import tpu as pltpu

D_MODEL = 1024
GRID_W = 64
ROPE_THETA = 10000.0
EPS = 1e-6

MLA_HEADS = 8
MLA_NOPE_DIM = 64
MLA_ROPE_DIM = 32
MLA_V_DIM = 64
Q_LORA_RANK = 256
KV_LORA_RANK = 128

GQA_HEADS = 8
GQA_KV_HEADS = 2
GQA_HEAD_DIM = 64
GQA_GROUP = GQA_HEADS // GQA_KV_HEADS

D_FF = 2816
N_MOD = 9

N_Q_HEADS = MLA_HEADS + GQA_HEADS
N_KV_HEADS = MLA_HEADS + GQA_KV_HEADS
HEAD_PAD = 128
V_DIM = 64
BF16_SUBLANES = 16
V_ROWS = V_DIM + BF16_SUBLANES
LOG2E = math.log2(math.e)

_OFF = np.cumsum([0, Q_LORA_RANK, KV_LORA_RANK, MLA_ROPE_DIM,
                  GQA_HEADS * GQA_HEAD_DIM, GQA_KV_HEADS * GQA_HEAD_DIM, GQA_KV_HEADS * GQA_HEAD_DIM])
D_IN = int(_OFF[-1])

VMEM_LIMIT_BYTES = 56 * 1024 * 1024

TM_FFN = 1024
TM_MIX = 512
MXU_TILE = 256
FF_CHUNK_TILES = (6, 5)
KV_CHUNK = 512

_bf16 = jnp.bfloat16
_f32 = jnp.float32


def _dot(a, b):
    return jnp.dot(a, b, preferred_element_type=_f32)


def _dot_nt(a, b):
    return lax.dot_general(a, b, (((1,), (1,)), ((), ())), preferred_element_type=_f32)


def _dot_tn(a, b):
    return lax.dot_general(a, b, (((0,), (0,)), ((), ())), preferred_element_type=_f32)


def _const_spec(shape):
    return pl.BlockSpec(shape, lambda *_: (0,) * len(shape), pipeline_mode=pl.Buffered(1))


def _ada_kernel(c_ref, w_ref, b_ref, o_ref):
    c = c_ref[...]
    ca = c * jax.nn.sigmoid(c)
    w = w_ref[...]
    ca_hi = ca.astype(_bf16)
    ca_lo = (ca - ca_hi.astype(_f32)).astype(_bf16)
    w_hi = w.astype(_bf16)
    w_lo = (w - w_hi.astype(_f32)).astype(_bf16)
    o_ref[...] = _dot(ca_hi, w_hi) + (_dot(ca_hi, w_lo) + _dot(ca_lo, w_hi)) + b_ref[...]


def _ada(c, w_ada, b_ada):
    bsz, d = c.shape
    n = w_ada.shape[1]
    tn = 1152
    return pl.pallas_call(
        _ada_kernel,
        grid=(n // tn,),
        in_specs=[pl.BlockSpec((bsz, d), lambda j: (0, 0)),
                  pl.BlockSpec((d, tn), lambda j: (0, j)),
                  pl.BlockSpec((1, tn), lambda j: (0, j))],
        out_specs=pl.BlockSpec((bsz, tn), lambda j: (0, j)),
        out_shape=jax.ShapeDtypeStruct((bsz, n), _f32),
        compiler_params=pltpu.CompilerParams(vmem_limit_bytes=VMEM_LIMIT_BYTES),
        name="ada",
    )(c, w_ada, b_ada.reshape(1, n))


def _norm_mod(x, g, shift, scale):
    r = lax.rsqrt(jnp.mean(x * x, axis=-1, keepdims=True) + EPS)
    return (x * r * g) * (1.0 + scale) + shift


def _swiglu(h, wgu_ref, wd_ref):
    acc = None
    lo = 0
    for tiles in FF_CHUNK_TILES:
        hi = lo + tiles * MXU_TILE
        a = _dot(h, wgu_ref[:, lo:hi])
        b = _dot(h, wgu_ref[:, D_FF + lo:D_FF + hi])
        act = (a * jax.nn.sigmoid(a) * b).astype(_bf16)
        part = _dot(act, wd_ref[lo:hi, :])
        acc = part if acc is None else acc + part
        lo = hi
    assert lo == D_FF
    return acc


def _ffn_kernel(x_ref, mod_ref, g_ref, wgu_ref, wd_ref, o_ref, *, mod_base):
    shift = mod_ref[0, mod_base:mod_base + 1, :]
    scale = mod_ref[0, mod_base + 1:mod_base + 2, :]
    gate = mod_ref[0, mod_base + 2:mod_base + 3, :]
    x = x_ref[...]
    h = _norm_mod(x, g_ref[...], shift, scale).astype(_bf16)
    o_ref[...] = x + (0.5 * gate) * _swiglu(h, wgu_ref, wd_ref)


def _ffn(x2d, mod3, g, wgu, wd, seq, mod_base):
    n, d = x2d.shape
    tm = TM_FFN
    tpb = seq // tm
    return pl.pallas_call(
        functools.partial(_ffn_kernel, mod_base=mod_base),
        grid=(n // tm,),
        in_specs=[pl.BlockSpec((tm, d), lambda i: (i, 0)),
                  pl.BlockSpec((1, N_MOD, d), lambda i: (i // tpb, 0, 0)),
                  _const_spec((1, d)),
                  _const_spec(wgu.shape),
                  _const_spec(wd.shape)],
        out_specs=pl.BlockSpec((tm, d), lambda i: (i, 0)),
        out_shape=jax.ShapeDtypeStruct((n, d), _f32),
        compiler_params=pltpu.CompilerParams(vmem_limit_bytes=VMEM_LIMIT_BYTES),
        name="ffn1",
    )(x2d, mod3, g, wgu, wd)


def _rope_fm(x, tab):
    q = x.shape[0] // 4
    x1r, x2r, x1c, x2c = x[0:q], x[q:2 * q], x[2 * q:3 * q], x[3 * q:4 * q]
    cr, sr, cc, sc = tab[0:q], tab[q:2 * q], tab[2 * q:3 * q], tab[3 * q:4 * q]
    return jnp.concatenate([x1r * cr - x2r * sr, x1r * sr + x2r * cr,
                            x1c * cc - x2c * sc, x1c * sc + x2c * cc], axis=0)


def _rms_fm(x, g_col, extra=1.0):
    r = lax.rsqrt(jnp.mean(x * x, axis=0, keepdims=True) + EPS)
    if extra != 1.0:
        r = r * extra
    return x * r * g_col


def _mix_pre_kernel(x_ref, mod_ref, g_ref, winT_ref, gq_ref, wuqT_ref, gkv_ref, wukvT_ref,
                    gqh_ref, gkh_ref, t64_ref, t32_ref, qT_ref, k_ref, vT_ref):
    tm = x_ref.shape[0]
    x = x_ref[...]
    h = _norm_mod(x, g_ref[...], mod_ref[0, 3:4, :], mod_ref[0, 4:5, :]).astype(_bf16)
    zT = _dot_nt(winT_ref[...], h)
    q_lat = zT[_OFF[0]:_OFF[1]]
    kv_lat = zT[_OFF[1]:_OFF[2]]
    k_rope = zT[_OFF[2]:_OFF[3]]
    q_g = zT[_OFF[3]:_OFF[4]]
    k_g = zT[_OFF[4]:_OFF[5]]
    v_g = zT[_OFF[5]:_OFF[6]]
    t64 = t64_ref[...]
    t32 = t32_ref[...]

    zeros32 = jnp.zeros((HEAD_PAD - MLA_NOPE_DIM - MLA_ROPE_DIM, tm), _f32)
    zeros64 = jnp.zeros((HEAD_PAD - GQA_HEAD_DIM, tm), _f32)
    ones_pad = (lax.broadcasted_iota(jnp.int32, (V_ROWS - V_DIM, tm), 0) == 0).astype(_f32)

    mla_scale = (MLA_NOPE_DIM + MLA_ROPE_DIM) ** -0.5 * LOG2E
    qn = _rms_fm(q_lat, gq_ref[...], mla_scale).astype(_bf16)
    qa = _dot(wuqT_ref[...], qn)
    nope_rows = MLA_HEADS * MLA_NOPE_DIM
    for hd in range(MLA_HEADS):
        nope = qa[hd * MLA_NOPE_DIM:(hd + 1) * MLA_NOPE_DIM]
        pe = _rope_fm(qa[nope_rows + hd * MLA_ROPE_DIM:nope_rows + (hd + 1) * MLA_ROPE_DIM], t32)
        qT_ref[0, hd, 0] = jnp.concatenate([nope, pe, zeros32], axis=0).astype(_bf16)

    kvn = _rms_fm(kv_lat, gkv_ref[...]).astype(_bf16)
    kva = _dot(wukvT_ref[...], kvn)
    k_pe = _rope_fm(k_rope, t32)
    for hd in range(MLA_HEADS):
        k_nope = kva[hd * MLA_NOPE_DIM:(hd + 1) * MLA_NOPE_DIM]
        kT = jnp.concatenate([k_nope, k_pe, zeros32], axis=0)
        k_ref[0, hd] = kT.T.astype(_bf16)
        v = kva[nope_rows + hd * MLA_V_DIM:nope_rows + (hd + 1) * MLA_V_DIM]
        vT_ref[0, hd] = jnp.concatenate([v, ones_pad], axis=0).astype(_bf16)

    gqa_scale = GQA_HEAD_DIM ** -0.5 * LOG2E
    for hd in range(GQA_HEADS):
        xh = _rms_fm(q_g[hd * GQA_HEAD_DIM:(hd + 1) * GQA_HEAD_DIM], gqh_ref[...], gqa_scale)
        qT_ref[0, MLA_HEADS + hd, 0] = jnp.concatenate([_rope_fm(xh, t64), zeros64], axis=0).astype(_bf16)
    for hd in range(GQA_KV_HEADS):
        xh = _rms_fm(k_g[hd * GQA_HEAD_DIM:(hd + 1) * GQA_HEAD_DIM], gkh_ref[...])
        kT = jnp.concatenate([_rope_fm(xh, t64), zeros64], axis=0)
        k_ref[0, MLA_HEADS + hd] = kT.T.astype(_bf16)
        v = v_g[hd * GQA_HEAD_DIM:(hd + 1) * GQA_HEAD_DIM]
        vT_ref[0, MLA_HEADS + hd] = jnp.concatenate([v, ones_pad], axis=0).astype(_bf16)


def _mix_pre(x2d, mod3, g, winT, gq, wuqT, gkv, wukvT, gqh, gkh, t64, t32, bsz, seq):
    n, d = x2d.shape
    tm = TM_MIX
    tpb = seq // tm
    out_shape = (jax.ShapeDtypeStruct((bsz, N_Q_HEADS, tpb, HEAD_PAD, tm), _bf16),
                 jax.ShapeDtypeStruct((bsz, N_KV_HEADS, seq, HEAD_PAD), _bf16),
                 jax.ShapeDtypeStruct((bsz, N_KV_HEADS, V_ROWS, seq), _bf16))
    return pl.pallas_call(
        _mix_pre_kernel,
        grid=(n // tm,),
        in_specs=[pl.BlockSpec((tm, d), lambda i: (i, 0)),
                  pl.BlockSpec((1, N_MOD, d), lambda i: (i // tpb, 0, 0)),
                  _const_spec((1, d)),
                  _const_spec(winT.shape),
                  _const_spec(gq.shape),
                  _const_spec(wuqT.shape),
                  _const_spec(gkv.shape),
                  _const_spec(wukvT.shape),
                  _const_spec(gqh.shape),
                  _const_spec(gkh.shape),
                  pl.BlockSpec((GQA_HEAD_DIM, tm), lambda i: (0, i % tpb)),
                  pl.BlockSpec((MLA_ROPE_DIM, tm), lambda i: (0, i % tpb))],
        out_specs=(pl.BlockSpec((1, N_Q_HEADS, 1, HEAD_PAD, tm), lambda i: (i // tpb, 0, i % tpb, 0, 0)),
                   pl.BlockSpec((1, N_KV_HEADS, tm, HEAD_PAD), lambda i: (i // tpb, 0, i % tpb, 0)),
                   pl.BlockSpec((1, N_KV_HEADS, V_ROWS, tm), lambda i: (i // tpb, 0, 0, i % tpb))),
        out_shape=out_shape,
        compiler_params=pltpu.CompilerParams(vmem_limit_bytes=VMEM_LIMIT_BYTES),
        name="mix_pre",
    )(x2d, mod3, g, winT, gq, wuqT, gkv, wukvT, gqh, gkh, t64, t32)


def _attn_kernel(q_ref, qn_ref, k_ref, kn_ref, vT_ref, o_ref, s_even, s_odd, m_even, m_odd):
    n_tiles = q_ref.shape[2]
    seq, tq = s_even.shape
    n_chunks = seq // KV_CHUNK

    def score_chunk(q, kk_ref, s_w, c, m_run):
        rows = slice(c * KV_CHUNK, (c + 1) * KV_CHUNK)
        s = _dot(kk_ref[0, 0, rows, :], q)
        s_w[rows, :] = s
        mc = jnp.max(s.reshape(KV_CHUNK // 8, 8, tq), axis=0)
        return mc if m_run is None else jnp.maximum(m_run, mc)

    def reduce_chunk(s_r, m, c, acc):
        rows = slice(c * KV_CHUNK, (c + 1) * KV_CHUNK)
        p = jnp.exp2(s_r[rows, :] - m).astype(_bf16)
        part = _dot(vT_ref[0, 0, :, rows], p)
        return part if acc is None else acc + part

    def fused(q, kk_ref, s_w, m_w, s_r, m_r, t_out):
        m = m_r[...]
        m_run, acc = None, None
        for c in range(n_chunks):
            m_run = score_chunk(q, kk_ref, s_w, c, m_run)
            acc = reduce_chunk(s_r, m, c, acc)
        m_w[...] = jnp.max(m_run, axis=0, keepdims=True)
        o_ref[0, 0, t_out] = acc[0:V_DIM] * (1.0 / acc[V_DIM:V_DIM + 1])

    @pl.when(pl.program_id(0) == 0)
    def _():
        m_run = None
        for c in range(n_chunks):
            m_run = score_chunk(q_ref[0, 0, 0], k_ref, s_even, c, m_run)
        m_even[...] = jnp.max(m_run, axis=0, keepdims=True)

    def pair(j, carry):
        t = 2 * j
        fused(q_ref[0, 0, t + 1], k_ref, s_odd, m_odd, s_even, m_even, t)
        fused(q_ref[0, 0, t + 2], k_ref, s_even, m_even, s_odd, m_odd, t + 1)
        return carry

    lax.fori_loop(0, n_tiles // 2 - 1, pair, 0)
    fused(q_ref[0, 0, n_tiles - 1], k_ref, s_odd, m_odd, s_even, m_even, n_tiles - 2)
    fused(qn_ref[0, 0, 0], kn_ref, s_even, m_even, s_odd, m_odd, n_tiles - 1)


def _kv_head(h):
    return jnp.where(h < MLA_HEADS, h, MLA_HEADS + (h - MLA_HEADS) // GQA_GROUP)


def _attention(qT, k, vT):
    bsz, _, n_tiles, _, tq = qT.shape
    seq = n_tiles * tq
    assert n_tiles % 2 == 0
    n_steps = bsz * N_Q_HEADS

    def cur(g):
        return g // N_Q_HEADS, g % N_Q_HEADS

    def nxt(g):
        return cur(jnp.minimum(g + 1, n_steps - 1))

    def q_map(bh):
        return lambda g: (*bh(g), 0, 0, 0)

    def kv_map(bh):
        def index(g):
            b, h = bh(g)
            return b, _kv_head(h), 0, 0
        return index

    return pl.pallas_call(
        _attn_kernel,
        grid=(n_steps,),
        in_specs=[pl.BlockSpec((1, 1, n_tiles, HEAD_PAD, tq), q_map(cur)),
                  pl.BlockSpec((1, 1, 1, HEAD_PAD, tq), q_map(nxt)),
                  pl.BlockSpec((1, 1, seq, HEAD_PAD), kv_map(cur)),
                  pl.BlockSpec((1, 1, seq, HEAD_PAD), kv_map(nxt)),
                  pl.BlockSpec((1, 1, V_ROWS, seq), kv_map(cur))],
        out_specs=pl.BlockSpec((1, 1, n_tiles, V_DIM, tq), q_map(cur)),
        out_shape=jax.ShapeDtypeStruct((bsz, N_Q_HEADS, n_tiles, V_DIM, tq), _f32),
        scratch_shapes=[pltpu.VMEM((seq, tq), _f32), pltpu.VMEM((seq, tq), _f32),
                        pltpu.VMEM((1, tq), _f32), pltpu.VMEM((1, tq), _f32)],
        compiler_params=pltpu.CompilerParams(dimension_semantics=("arbitrary",),
                                             vmem_limit_bytes=VMEM_LIMIT_BYTES),
        name="attn",
    )(qT, qT, k, k, vT)


def _post_kernel(x_ref, o_ref_in, mod_ref, gom_ref, gog_ref, wout_ref, g2_ref, wgu_ref, wd_ref, gf_ref, out_ref,
                 *, final_norm):
    tm = x_ref.shape[0]
    x = x_ref[...]
    oT = o_ref_in[0, :, 0].reshape(N_Q_HEADS * V_DIM, tm)
    half = MLA_HEADS * MLA_V_DIM
    on = jnp.concatenate([_rms_fm(oT[:half], gom_ref[...]), _rms_fm(oT[half:], gog_ref[...])], axis=0)
    y = _dot_tn(on.astype(_bf16), wout_ref[...])
    x = x + mod_ref[0, 5:6, :] * y
    h = _norm_mod(x, g2_ref[...], mod_ref[0, 6:7, :], mod_ref[0, 7:8, :]).astype(_bf16)
    x = x + (0.5 * mod_ref[0, 8:9, :]) * _swiglu(h, wgu_ref, wd_ref)
    if final_norm:
        r = lax.rsqrt(jnp.mean(x * x, axis=-1, keepdims=True) + EPS)
        x = x * r * gf_ref[...]
    out_ref[...] = x


def _post(x2d, oT, mod3, gom, gog, wout, g2, wgu, wd, gf, seq, final_norm):
    n, d = x2d.shape
    tm = oT.shape[-1]
    tpb = seq // tm
    return pl.pallas_call(
        functools.partial(_post_kernel, final_norm=final_norm),
        grid=(n // tm,),
        in_specs=[pl.BlockSpec((tm, d), lambda i: (i, 0)),
                  pl.BlockSpec((1, N_Q_HEADS, 1, V_DIM, tm), lambda i: (i // tpb, 0, i % tpb, 0, 0)),
                  pl.BlockSpec((1, N_MOD, d), lambda i: (i // tpb, 0, 0)),
                  _const_spec(gom.shape),
                  _const_spec(gog.shape),
                  _const_spec(wout.shape),
                  _const_spec((1, d)),
                  _const_spec(wgu.shape),
                  _const_spec(wd.shape),
                  _const_spec((1, d))],
        out_specs=pl.BlockSpec((tm, d), lambda i: (i, 0)),
        out_shape=jax.ShapeDtypeStruct((n, d), _f32),
        compiler_params=pltpu.CompilerParams(vmem_limit_bytes=VMEM_LIMIT_BYTES),
        name="post",
    )(x2d, oT, mod3, gom, gog, wout, g2, wgu, wd, gf)


def _rope_table(seq, dim):
    rows = seq // GRID_W
    row = jnp.repeat(jnp.arange(rows), GRID_W).astype(_f32)
    col = jnp.tile(jnp.arange(GRID_W), rows).astype(_f32)
    axis_dim = dim // 2
    inv_freq = ROPE_THETA ** (-(jnp.arange(axis_dim // 2, dtype=_f32) * 2.0 / axis_dim))
    ang_row = row[:, None] * inv_freq[None, :]
    ang_col = col[:, None] * inv_freq[None, :]
    return jnp.concatenate([jnp.cos(ang_row), jnp.sin(ang_row), jnp.cos(ang_col), jnp.sin(ang_col)], axis=1).T


def kernel(x, c, w_ada, b_ada, g_ffn1, w1_gu, w1_down, g_mix, w_in, g_q_lat, w_uq, g_kv_lat, w_ukv,
           g_qhead, g_khead, g_out_mla, g_out_gqa, w_out, g_ffn2, w2_gu, w2_down, g_final):
    bsz, seq, d = x.shape
    depth = w_ada.shape[0]
    t64 = _rope_table(seq, GQA_HEAD_DIM)
    t32 = _rope_table(seq, MLA_ROPE_DIM)
    x2d = x.reshape(bsz * seq, d)
    for l in range(depth):
        mod3 = _ada(c, w_ada[l], b_ada[l]).reshape(bsz, N_MOD, d)

        x2d = _ffn(x2d, mod3, g_ffn1[l].reshape(1, d), w1_gu[l].astype(_bf16), w1_down[l].astype(_bf16), seq, 0)

        winT = w_in[l].T.astype(_bf16)
        wuq = w_uq[l].reshape(Q_LORA_RANK, MLA_HEADS, MLA_NOPE_DIM + MLA_ROPE_DIM)
        wuqT = jnp.concatenate([wuq[:, :, :MLA_NOPE_DIM].reshape(Q_LORA_RANK, -1),
                                wuq[:, :, MLA_NOPE_DIM:].reshape(Q_LORA_RANK, -1)], axis=1).T.astype(_bf16)
        wukv = w_ukv[l].reshape(KV_LORA_RANK, MLA_HEADS, MLA_NOPE_DIM + MLA_V_DIM)
        wukvT = jnp.concatenate([wukv[:, :, :MLA_NOPE_DIM].reshape(KV_LORA_RANK, -1),
                                 wukv[:, :, MLA_NOPE_DIM:].reshape(KV_LORA_RANK, -1)], axis=1).T.astype(_bf16)
        qT, k, vT = _mix_pre(x2d, mod3, g_mix[l].reshape(1, d), winT,
                             g_q_lat[l].reshape(-1, 1), wuqT, g_kv_lat[l].reshape(-1, 1), wukvT,
                             g_qhead[l].reshape(-1, 1), g_khead[l].reshape(-1, 1), t64, t32, bsz, seq)

        oT = _attention(qT, k, vT)

        x2d = _post(x2d, oT, mod3, g_out_mla[l].reshape(-1, 1), g_out_gqa[l].reshape(-1, 1),
                    w_out[l].astype(_bf16), g_ffn2[l].reshape(1, d),
                    w2_gu[l].astype(_bf16), w2_down[l].astype(_bf16), g_final.reshape(1, d), seq,
                    final_norm=(l == depth - 1))
    return x2d.reshape(bsz, seq, d)
```

```python
import functools
import math

import jax
import jax.numpy as jnp
import numpy as np
from jax import lax
from jax.experimental import pallas as pl
from jax.experimental.pallas import tpu as pltpu

D_MODEL = 1024
GRID_W = 64
ROPE_THETA = 10000.0
EPS = 1e-6

MLA_HEADS = 8
MLA_NOPE_DIM = 64
MLA_ROPE_DIM = 32
MLA_V_DIM = 64
Q_LORA_RANK = 256
KV_LORA_RANK = 128

GQA_HEADS = 8
GQA_KV_HEADS = 2
GQA_HEAD_DIM = 64
GQA_GROUP = GQA_HEADS // GQA_KV_HEADS

D_FF = 2816
N_MOD = 9

N_Q_HEADS = MLA_HEADS + GQA_HEADS
N_KV_HEADS = MLA_HEADS + GQA_KV_HEADS
HEAD_PAD = 128
V_DIM = 64
BF16_SUBLANES = 16
V_ROWS = V_DIM + BF16_SUBLANES
LOG2E = math.log2(math.e)

_OFF = np.cumsum([0, Q_LORA_RANK, KV_LORA_RANK, MLA_ROPE_DIM,
                  GQA_HEADS * GQA_HEAD_DIM, GQA_KV_HEADS * GQA_HEAD_DIM, GQA_KV_HEADS * GQA_HEAD_DIM])
D_IN = int(_OFF[-1])

VMEM_LIMIT_BYTES = 56 * 1024 * 1024

TM_FFN = 1024
TM_MIX = 512
MXU_TILE = 256
FF_CHUNK_TILES = (6, 5)
KV_CHUNK = 256

_bf16 = jnp.bfloat16
_f32 = jnp.float32


def _dot(a, b):
    return jnp.dot(a, b, preferred_element_type=_f32)


def _dot_nt(a, b):
    return lax.dot_general(a, b, (((1,), (1,)), ((), ())), preferred_element_type=_f32)


def _dot_tn(a, b):
    return lax.dot_general(a, b, (((0,), (0,)), ((), ())), preferred_element_type=_f32)


def _const_spec(shape):
    return pl.BlockSpec(shape, lambda *_: (0,) * len(shape), pipeline_mode=pl.Buffered(1))


def _ada_kernel(c_ref, w_ref, b_ref, o_ref):
    c = c_ref[...]
    ca = c * jax.nn.sigmoid(c)
    w = w_ref[...]
    ca_hi = ca.astype(_bf16)
    ca_lo = (ca - ca_hi.astype(_f32)).astype(_bf16)
    w_hi = w.astype(_bf16)
    w_lo = (w - w_hi.astype(_f32)).astype(_bf16)
    o_ref[...] = _dot(ca_hi, w_hi) + (_dot(ca_hi, w_lo) + _dot(ca_lo, w_hi)) + b_ref[...]


def _ada(c, w_ada, b_ada):
    bsz, d = c.shape
    n = w_ada.shape[1]
    tn = 1152
    return pl.pallas_call(
        _ada_kernel,
        grid=(n // tn,),
        in_specs=[pl.BlockSpec((bsz, d), lambda j: (0, 0)),
                  pl.BlockSpec((d, tn), lambda j: (0, j)),
                  pl.BlockSpec((1, tn), lambda j: (0, j))],
        out_specs=pl.BlockSpec((bsz, tn), lambda j: (0, j)),
        out_shape=jax.ShapeDtypeStruct((bsz, n), _f32),
        compiler_params=pltpu.CompilerParams(vmem_limit_bytes=VMEM_LIMIT_BYTES),
        name="ada",
    )(c, w_ada, b_ada.reshape(1, n))


def _norm_mod(x, g, shift, scale):
    r = lax.rsqrt(jnp.mean(x * x, axis=-1, keepdims=True) + EPS)
    return (x * r * g) * (1.0 + scale) + shift


def _swiglu(h, wgu_ref, wd_ref):
    acc = None
    lo = 0
    for tiles in FF_CHUNK_TILES:
        hi = lo + tiles * MXU_TILE
        a = _dot(h, wgu_ref[:, lo:hi])
        b = _dot(h, wgu_ref[:, D_FF + lo:D_FF + hi])
        act = (a * jax.nn.sigmoid(a) * b).astype(_bf16)
        part = _dot(act, wd_ref[lo:hi, :])
        acc = part if acc is None else acc + part
        lo = hi
    assert lo == D_FF
    return acc


def _ffn_kernel(x_ref, mod_ref, g_ref, wgu_ref, wd_ref, o_ref, *, mod_base):
    shift = mod_ref[0, mod_base:mod_base + 1, :]
    scale = mod_ref[0, mod_base + 1:mod_base + 2, :]
    gate = mod_ref[0, mod_base + 2:mod_base + 3, :]
    x = x_ref[...]
    h = _norm_mod(x, g_ref[...], shift, scale).astype(_bf16)
    o_ref[...] = x + (0.5 * gate) * _swiglu(h, wgu_ref, wd_ref)


def _ffn(x2d, mod3, g, wgu, wd, seq, mod_base):
    n, d = x2d.shape
    tm = TM_FFN
    tpb = seq // tm
    return pl.pallas_call(
        functools.partial(_ffn_kernel, mod_base=mod_base),
        grid=(n // tm,),
        in_specs=[pl.BlockSpec((tm, d), lambda i: (i, 0)),
                  pl.BlockSpec((1, N_MOD, d), lambda i: (i // tpb, 0, 0)),
                  _const_spec((1, d)),
                  _const_spec(wgu.shape),
                  _const_spec(wd.shape)],
        out_specs=pl.BlockSpec((tm, d), lambda i: (i, 0)),
        out_shape=jax.ShapeDtypeStruct((n, d), _f32),
        compiler_params=pltpu.CompilerParams(vmem_limit_bytes=VMEM_LIMIT_BYTES),
        name="ffn1",
    )(x2d, mod3, g, wgu, wd)


def _rope_fm(x, tab):
    q = x.shape[0] // 4
    x1r, x2r, x1c, x2c = x[0:q], x[q:2 * q], x[2 * q:3 * q], x[3 * q:4 * q]
    cr, sr, cc, sc = tab[0:q], tab[q:2 * q], tab[2 * q:3 * q], tab[3 * q:4 * q]
    return jnp.concatenate([x1r * cr - x2r * sr, x1r * sr + x2r * cr,
                            x1c * cc - x2c * sc, x1c * sc + x2c * cc], axis=0)


def _rms_fm(x, g_col, extra=1.0):
    r = lax.rsqrt(jnp.mean(x * x, axis=0, keepdims=True) + EPS)
    if extra != 1.0:
        r = r * extra
    return x * r * g_col


def _mix_pre_kernel(x_ref, mod_ref, g_ref, winT_ref, gq_ref, wuqT_ref, gkv_ref, wukvT_ref,
                    gqh_ref, gkh_ref, t64_ref, t32_ref, qT_ref, k_ref, vT_ref):
    tm = x_ref.shape[0]
    x = x_ref[...]
    h = _norm_mod(x, g_ref[...], mod_ref[0, 3:4, :], mod_ref[0, 4:5, :]).astype(_bf16)
    zT = _dot_nt(winT_ref[...], h)
    q_lat = zT[_OFF[0]:_OFF[1]]
    kv_lat = zT[_OFF[1]:_OFF[2]]
    k_rope = zT[_OFF[2]:_OFF[3]]
    q_g = zT[_OFF[3]:_OFF[4]]
    k_g = zT[_OFF[4]:_OFF[5]]
    v_g = zT[_OFF[5]:_OFF[6]]
    t64 = t64_ref[...]
    t32 = t32_ref[...]

    zeros32 = jnp.zeros((HEAD_PAD - MLA_NOPE_DIM - MLA_ROPE_DIM, tm), _f32)
    zeros64 = jnp.zeros((HEAD_PAD - GQA_HEAD_DIM, tm), _f32)
    ones_pad = (lax.broadcasted_iota(jnp.int32, (V_ROWS - V_DIM, tm), 0) == 0).astype(_f32)

    mla_scale = (MLA_NOPE_DIM + MLA_ROPE_DIM) ** -0.5 * LOG2E
    qn = _rms_fm(q_lat, gq_ref[...], mla_scale).astype(_bf16)
    qa = _dot(wuqT_ref[...], qn)
    nope_rows = MLA_HEADS * MLA_NOPE_DIM
    for hd in range(MLA_HEADS):
        nope = qa[hd * MLA_NOPE_DIM:(hd + 1) * MLA_NOPE_DIM]
        pe = _rope_fm(qa[nope_rows + hd * MLA_ROPE_DIM:nope_rows + (hd + 1) * MLA_ROPE_DIM], t32)
        qT_ref[0, hd, 0] = jnp.concatenate([nope, pe, zeros32], axis=0).astype(_bf16)

    kvn = _rms_fm(kv_lat, gkv_ref[...]).astype(_bf16)
    kva = _dot(wukvT_ref[...], kvn)
    k_pe = _rope_fm(k_rope, t32)
    for hd in range(MLA_HEADS):
        k_nope = kva[hd * MLA_NOPE_DIM:(hd + 1) * MLA_NOPE_DIM]
        kT = jnp.concatenate([k_nope, k_pe, zeros32], axis=0)
        k_ref[0, hd] = kT.T.astype(_bf16)
        v = kva[nope_rows + hd * MLA_V_DIM:nope_rows + (hd + 1) * MLA_V_DIM]
        vT_ref[0, hd] = jnp.concatenate([v, ones_pad], axis=0).astype(_bf16)

    gqa_scale = GQA_HEAD_DIM ** -0.5 * LOG2E
    for hd in range(GQA_HEADS):
        xh = _rms_fm(q_g[hd * GQA_HEAD_DIM:(hd + 1) * GQA_HEAD_DIM], gqh_ref[...], gqa_scale)
        qT_ref[0, MLA_HEADS + hd, 0] = jnp.concatenate([_rope_fm(xh, t64), zeros64], axis=0).astype(_bf16)
    for hd in range(GQA_KV_HEADS):
        xh = _rms_fm(k_g[hd * GQA_HEAD_DIM:(hd + 1) * GQA_HEAD_DIM], gkh_ref[...])
        kT = jnp.concatenate([_rope_fm(xh, t64), zeros64], axis=0)
        k_ref[0, MLA_HEADS + hd] = kT.T.astype(_bf16)
        v = v_g[hd * GQA_HEAD_DIM:(hd + 1) * GQA_HEAD_DIM]
        vT_ref[0, MLA_HEADS + hd] = jnp.concatenate([v, ones_pad], axis=0).astype(_bf16)


def _mix_pre(x2d, mod3, g, winT, gq, wuqT, gkv, wukvT, gqh, gkh, t64, t32, bsz, seq):
    n, d = x2d.shape
    tm = TM_MIX
    tpb = seq // tm
    out_shape = (jax.ShapeDtypeStruct((bsz, N_Q_HEADS, tpb, HEAD_PAD, tm), _bf16),
                 jax.ShapeDtypeStruct((bsz, N_KV_HEADS, seq, HEAD_PAD), _bf16),
                 jax.ShapeDtypeStruct((bsz, N_KV_HEADS, V_ROWS, seq), _bf16))
    return pl.pallas_call(
        _mix_pre_kernel,
        grid=(n // tm,),
        in_specs=[pl.BlockSpec((tm, d), lambda i: (i, 0)),
                  pl.BlockSpec((1, N_MOD, d), lambda i: (i // tpb, 0, 0)),
                  _const_spec((1, d)),
                  _const_spec(winT.shape),
                  _const_spec(gq.shape),
                  _const_spec(wuqT.shape),
                  _const_spec(gkv.shape),
                  _const_spec(wukvT.shape),
                  _const_spec(gqh.shape),
                  _const_spec(gkh.shape),
                  pl.BlockSpec((GQA_HEAD_DIM, tm), lambda i: (0, i % tpb)),
                  pl.BlockSpec((MLA_ROPE_DIM, tm), lambda i: (0, i % tpb))],
        out_specs=(pl.BlockSpec((1, N_Q_HEADS, 1, HEAD_PAD, tm), lambda i: (i // tpb, 0, i % tpb, 0, 0)),
                   pl.BlockSpec((1, N_KV_HEADS, tm, HEAD_PAD), lambda i: (i // tpb, 0, i % tpb, 0)),
                   pl.BlockSpec((1, N_KV_HEADS, V_ROWS, tm), lambda i: (i // tpb, 0, 0, i % tpb))),
        out_shape=out_shape,
        compiler_params=pltpu.CompilerParams(vmem_limit_bytes=VMEM_LIMIT_BYTES),
        name="mix_pre",
    )(x2d, mod3, g, winT, gq, wuqT, gkv, wukvT, gqh, gkh, t64, t32)


def _attn_kernel(q_ref, qn_ref, k_ref, kn_ref, vT_ref, o_ref, s_even, s_odd, m_even, m_odd):
    n_tiles = q_ref.shape[2]
    seq, tq = s_even.shape
    n_chunks = seq // KV_CHUNK

    def score_chunk(q, kk_ref, s_w, c, m_run):
        rows = slice(c * KV_CHUNK, (c + 1) * KV_CHUNK)
        s = _dot(kk_ref[0, 0, rows, :], q)
        s_w[rows, :] = s
        mc = jnp.max(s.reshape(KV_CHUNK // 8, 8, tq), axis=0)
        return mc if m_run is None else jnp.maximum(m_run, mc)

    def reduce_chunk(s_r, m, c, acc):
        rows = slice(c * KV_CHUNK, (c + 1) * KV_CHUNK)
        p = jnp.exp2(s_r[rows, :] - m).astype(_bf16)
        part = _dot(vT_ref[0, 0, :, rows], p)
        return part if acc is None else acc + part

    def fused(q, kk_ref, s_w, m_w, s_r, m_r, t_out):
        m = m_r[...]
        m_run, acc = None, None
        for c in range(n_chunks):
            m_run = score_chunk(q, kk_ref, s_w, c, m_run)
            acc = reduce_chunk(s_r, m, c, acc)
        m_w[...] = jnp.max(m_run, axis=0, keepdims=True)
        o_ref[0, 0, t_out] = acc[0:V_DIM] * (1.0 / acc[V_DIM:V_DIM + 1])

    @pl.when(pl.program_id(0) == 0)
    def _():
        m_run = None
        for c in range(n_chunks):
            m_run = score_chunk(q_ref[0, 0, 0], k_ref, s_even, c, m_run)
        m_even[...] = jnp.max(m_run, axis=0, keepdims=True)

    def pair(j, carry):
        t = 2 * j
        fused(q_ref[0, 0, t + 1], k_ref, s_odd, m_odd, s_even, m_even, t)
        fused(q_ref[0, 0, t + 2], k_ref, s_even, m_even, s_odd, m_odd, t + 1)
        return carry

    lax.fori_loop(0, n_tiles // 2 - 1, pair, 0)
    fused(q_ref[0, 0, n_tiles - 1], k_ref, s_odd, m_odd, s_even, m_even, n_tiles - 2)
    fused(qn_ref[0, 0, 0], kn_ref, s_even, m_even, s_odd, m_odd, n_tiles - 1)


def _kv_head(h):
    return jnp.where(h < MLA_HEADS, h, MLA_HEADS + (h - MLA_HEADS) // GQA_GROUP)


def _attention(qT, k, vT):
    bsz, _, n_tiles, _, tq = qT.shape
    seq = n_tiles * tq
    assert n_tiles % 2 == 0
    n_steps = bsz * N_Q_HEADS

    def cur(g):
        return g // N_Q_HEADS, g % N_Q_HEADS

    def nxt(g):
        return cur(jnp.minimum(g + 1, n_steps - 1))

    def q_map(bh):
        return lambda g: (*bh(g), 0, 0, 0)

    def kv_map(bh):
        def index(g):
            b, h = bh(g)
            return b, _kv_head(h), 0, 0
        return index

    return pl.pallas_call(
        _attn_kernel,
        grid=(n_steps,),
        in_specs=[pl.BlockSpec((1, 1, n_tiles, HEAD_PAD, tq), q_map(cur)),
                  pl.BlockSpec((1, 1, 1, HEAD_PAD, tq), q_map(nxt)),
                  pl.BlockSpec((1, 1, seq, HEAD_PAD), kv_map(cur)),
                  pl.BlockSpec((1, 1, seq, HEAD_PAD), kv_map(nxt)),
                  pl.BlockSpec((1, 1, V_ROWS, seq), kv_map(cur))],
        out_specs=pl.BlockSpec((1, 1, n_tiles, V_DIM, tq), q_map(cur)),
        out_shape=jax.ShapeDtypeStruct((bsz, N_Q_HEADS, n_tiles, V_DIM, tq), _f32),
        scratch_shapes=[pltpu.VMEM((seq, tq), _f32), pltpu.VMEM((seq, tq), _f32),
                        pltpu.VMEM((1, tq), _f32), pltpu.VMEM((1, tq), _f32)],
        compiler_params=pltpu.CompilerParams(dimension_semantics=("arbitrary",),
                                             vmem_limit_bytes=VMEM_LIMIT_BYTES),
        name="attn",
    )(qT, qT, k, k, vT)


def _post_kernel(x_ref, o_ref_in, mod_ref, gom_ref, gog_ref, wout_ref, g2_ref, wgu_ref, wd_ref, gf_ref, out_ref,
                 *, final_norm):
    tm = x_ref.shape[0]
    x = x_ref[...]
    oT = o_ref_in[0, :, 0].reshape(N_Q_HEADS * V_DIM, tm)
    half = MLA_HEADS * MLA_V_DIM
    on = jnp.concatenate([_rms_fm(oT[:half], gom_ref[...]), _rms_fm(oT[half:], gog_ref[...])], axis=0)
    y = _dot_tn(on.astype(_bf16), wout_ref[...])
    x = x + mod_ref[0, 5:6, :] * y
    h = _norm_mod(x, g2_ref[...], mod_ref[0, 6:7, :], mod_ref[0, 7:8, :]).astype(_bf16)
    x = x + (0.5 * mod_ref[0, 8:9, :]) * _swiglu(h, wgu_ref, wd_ref)
    if final_norm:
        r = lax.rsqrt(jnp.mean(x * x, axis=-1, keepdims=True) + EPS)
        x = x * r * gf_ref[...]
    out_ref[...] = x


def _post(x2d, oT, mod3, gom, gog, wout, g2, wgu, wd, gf, seq, final_norm):
    n, d = x2d.shape
    tm = oT.shape[-1]
    tpb = seq // tm
    return pl.pallas_call(
        functools.partial(_post_kernel, final_norm=final_norm),
        grid=(n // tm,),
        in_specs=[pl.BlockSpec((tm, d), lambda i: (i, 0)),
                  pl.BlockSpec((1, N_Q_HEADS, 1, V_DIM, tm), lambda i: (i // tpb, 0, i % tpb, 0, 0)),
                  pl.BlockSpec((1, N_MOD, d), lambda i: (i // tpb, 0, 0)),
                  _const_spec(gom.shape),
                  _const_spec(gog.shape),
                  _const_spec(wout.shape),
                  _const_spec((1, d)),
                  _const_spec(wgu.shape),
                  _const_spec(wd.shape),
                  _const_spec((1, d))],
        out_specs=pl.BlockSpec((tm, d), lambda i: (i, 0)),
        out_shape=jax.ShapeDtypeStruct((n, d), _f32),
        compiler_params=pltpu.CompilerParams(vmem_limit_bytes=VMEM_LIMIT_BYTES),
        name="post",
    )(x2d, oT, mod3, gom, gog, wout, g2, wgu, wd, gf)


def _rope_table(seq, dim):
    rows = seq // GRID_W
    row = jnp.repeat(jnp.arange(rows), GRID_W).astype(_f32)
    col = jnp.tile(jnp.arange(GRID_W), rows).astype(_f32)
    axis_dim = dim // 2
    inv_freq = ROPE_THETA ** (-(jnp.arange(axis_dim // 2, dtype=_f32) * 2.0 / axis_dim))
    ang_row = row[:, None] * inv_freq[None, :]
    ang_col = col[:, None] * inv_freq[None, :]
    return jnp.concatenate([jnp.cos(ang_row), jnp.sin(ang_row), jnp.cos(ang_col), jnp.sin(ang_col)], axis=1).T


def kernel(x, c, w_ada, b_ada, g_ffn1, w1_gu, w1_down, g_mix, w_in, g_q_lat, w_uq, g_kv_lat, w_ukv,
           g_qhead, g_khead, g_out_mla, g_out_gqa, w_out, g_ffn2, w2_gu, w2_down, g_final):
    bsz, seq, d = x.shape
    depth = w_ada.shape[0]
    t64 = _rope_table(seq, GQA_HEAD_DIM)
    t32 = _rope_table(seq, MLA_ROPE_DIM)
    x2d = x.reshape(bsz * seq, d)
    for l in range(depth):
        mod3 = _ada(c, w_ada[l], b_ada[l]).reshape(bsz, N_MOD, d)

        x2d = _ffn(x2d, mod3, g_ffn1[l].reshape(1, d), w1_gu[l].astype(_bf16), w1_down[l].astype(_bf16), seq, 0)

        winT = w_in[l].T.astype(_bf16)
        wuq = w_uq[l].reshape(Q_LORA_RANK, MLA_HEADS, MLA_NOPE_DIM + MLA_ROPE_DIM)
        wuqT = jnp.concatenate([wuq[:, :, :MLA_NOPE_DIM].reshape(Q_LORA_RANK, -1),
                                wuq[:, :, MLA_NOPE_DIM:].reshape(Q_LORA_RANK, -1)], axis=1).T.astype(_bf16)
        wukv = w_ukv[l].reshape(KV_LORA_RANK, MLA_HEADS, MLA_NOPE_DIM + MLA_V_DIM)
        wukvT = jnp.concatenate([wukv[:, :, :MLA_NOPE_DIM].reshape(KV_LORA_RANK, -1),
                                 wukv[:, :, MLA_NOPE_DIM:].reshape(KV_LORA_RANK, -1)], axis=1).T.astype(_bf16)
        qT, k, vT = _mix_pre(x2d, mod3, g_mix[l].reshape(1, d), winT,
                             g_q_lat[l].reshape(-1, 1), wuqT, g_kv_lat[l].reshape(-1, 1), wukvT,
                             g_qhead[l].reshape(-1, 1), g_khead[l].reshape(-1, 1), t64, t32, bsz, seq)

        oT = _attention(qT, k, vT)

        x2d = _post(x2d, oT, mod3, g_out_mla[l].reshape(-1, 1), g_out_gqa[l].reshape(-1, 1),
                    w_out[l].astype(_bf16), g_ffn2[l].reshape(1, d),
                    w2_gu[l].astype(_bf16), w2_down[l].astype(_bf16), g_final.reshape(1, d), seq,
                    final_norm=(l == depth - 1))
    return x2d.reshape(bsz, seq, d)
```

```python
import functools
import math

import jax
import jax.numpy as jnp
import numpy as np
from jax import lax
from jax.experimental import pallas as pl
from jax.experimental.pallas import tpu as pltpu

D_MODEL = 1024
GRID_W = 64
ROPE_THETA = 10000.0
EPS = 1e-6

MLA_HEADS = 8
MLA_NOPE_DIM = 64
MLA_ROPE_DIM = 32
MLA_V_DIM = 64
Q_LORA_RANK = 256
KV_LORA_RANK = 128

GQA_HEADS = 8
GQA_KV_HEADS = 2
GQA_HEAD_DIM = 64
GQA_GROUP = GQA_HEADS // GQA_KV_HEADS

D_FF = 2816
N_MOD = 9

N_Q_HEADS = MLA_HEADS + GQA_HEADS
N_KV_HEADS = MLA_HEADS + GQA_KV_HEADS
HEAD_PAD = 128
V_DIM = 64
BF16_SUBLANES = 16
V_ROWS = V_DIM + BF16_SUBLANES
LOG2E = math.log2(math.e)

_OFF = np.cumsum([0, Q_LORA_RANK, KV_LORA_RANK, MLA_ROPE_DIM,
                  GQA_HEADS * GQA_HEAD_DIM, GQA_KV_HEADS * GQA_HEAD_DIM, GQA_KV_HEADS * GQA_HEAD_DIM])
D_IN = int(_OFF[-1])

VMEM_LIMIT_BYTES = 56 * 1024 * 1024

TM_FFN = 1024
TM_MIX = 512
MXU_TILE = 256
FF_CHUNK_TILES = (6, 5)
KV_CHUNK = 256

_bf16 = jnp.bfloat16
_f32 = jnp.float32


def _dot(a, b):
    return jnp.dot(a, b, preferred_element_type=_f32)


def _dot_nt(a, b):
    return lax.dot_general(a, b, (((1,), (1,)), ((), ())), preferred_element_type=_f32)


def _dot_tn(a, b):
    return lax.dot_general(a, b, (((0,), (0,)), ((), ())), preferred_element_type=_f32)


def _const_spec(shape):
    return pl.BlockSpec(shape, lambda *_: (0,) * len(shape), pipeline_mode=pl.Buffered(1))


def _ada_kernel(c_ref, w_ref, b_ref, o_ref):
    c = c_ref[...]
    ca = c * jax.nn.sigmoid(c)
    w = w_ref[...]
    ca_hi = ca.astype(_bf16)
    ca_lo = (ca - ca_hi.astype(_f32)).astype(_bf16)
    w_hi = w.astype(_bf16)
    w_lo = (w - w_hi.astype(_f32)).astype(_bf16)
    o_ref[...] = _dot(ca_hi, w_hi) + (_dot(ca_hi, w_lo) + _dot(ca_lo, w_hi)) + b_ref[...]


def _ada(c, w_ada, b_ada):
    bsz, d = c.shape
    n = w_ada.shape[1]
    tn = 1152
    return pl.pallas_call(
        _ada_kernel,
        grid=(n // tn,),
        in_specs=[pl.BlockSpec((bsz, d), lambda j: (0, 0)),
                  pl.BlockSpec((d, tn), lambda j: (0, j)),
                  pl.BlockSpec((1, tn), lambda j: (0, j))],
        out_specs=pl.BlockSpec((bsz, tn), lambda j: (0, j)),
        out_shape=jax.ShapeDtypeStruct((bsz, n), _f32),
        compiler_params=pltpu.CompilerParams(vmem_limit_bytes=VMEM_LIMIT_BYTES),
        name="ada",
    )(c, w_ada, b_ada.reshape(1, n))


def _norm_mod(x, g, shift, scale):
    r = lax.rsqrt(jnp.mean(x * x, axis=-1, keepdims=True) + EPS)
    return (x * r * g) * (1.0 + scale) + shift


def _swiglu(h, wgu_ref, wd_ref):
    acc = None
    lo = 0
    for tiles in FF_CHUNK_TILES:
        hi = lo + tiles * MXU_TILE
        a = _dot(h, wgu_ref[:, lo:hi])
        b = _dot(h, wgu_ref[:, D_FF + lo:D_FF + hi])
        act = (a * jax.nn.sigmoid(a) * b).astype(_bf16)
        part = _dot(act, wd_ref[lo:hi, :])
        acc = part if acc is None else acc + part
        lo = hi
    assert lo == D_FF
    return acc


def _ffn_kernel(x_ref, mod_ref, g_ref, wgu_ref, wd_ref, o_ref, *, mod_base):
    shift = mod_ref[0, mod_base:mod_base + 1, :]
    scale = mod_ref[0, mod_base + 1:mod_base + 2, :]
    gate = mod_ref[0, mod_base + 2:mod_base + 3, :]
    x = x_ref[...]
    h = _norm_mod(x, g_ref[...], shift, scale).astype(_bf16)
    o_ref[...] = x + (0.5 * gate) * _swiglu(h, wgu_ref, wd_ref)


def _ffn(x2d, mod3, g, wgu, wd, seq, mod_base):
    n, d = x2d.shape
    tm = TM_FFN
    tpb = seq // tm
    return pl.pallas_call(
        functools.partial(_ffn_kernel, mod_base=mod_base),
        grid=(n // tm,),
        in_specs=[pl.BlockSpec((tm, d), lambda i: (i, 0)),
                  pl.BlockSpec((1, N_MOD, d), lambda i: (i // tpb, 0, 0)),
                  _const_spec((1, d)),
                  _const_spec(wgu.shape),
                  _const_spec(wd.shape)],
        out_specs=pl.BlockSpec((tm, d), lambda i: (i, 0)),
        out_shape=jax.ShapeDtypeStruct((n, d), _f32),
        compiler_params=pltpu.CompilerParams(vmem_limit_bytes=VMEM_LIMIT_BYTES),
        name="ffn1",
    )(x2d, mod3, g, wgu, wd)


def _rope_fm(x, tab):
    q = x.shape[0] // 4
    x1r, x2r, x1c, x2c = x[0:q], x[q:2 * q], x[2 * q:3 * q], x[3 * q:4 * q]
    cr, sr, cc, sc = tab[0:q], tab[q:2 * q], tab[2 * q:3 * q], tab[3 * q:4 * q]
    return jnp.concatenate([x1r * cr - x2r * sr, x1r * sr + x2r * cr,
                            x1c * cc - x2c * sc, x1c * sc + x2c * cc], axis=0)


def _rms_fm(x, g_col, extra=1.0):
    r = lax.rsqrt(jnp.mean(x * x, axis=0, keepdims=True) + EPS)
    if extra != 1.0:
        r = r * extra
    return x * r * g_col


def _mix_pre_kernel(x_ref, mod_ref, g_ref, winT_ref, gq_ref, wuqT_ref, gkv_ref, wukvT_ref,
                    gqh_ref, gkh_ref, t64_ref, t32_ref, qT_ref, k_ref, vT_ref):
    tm = x_ref.shape[0]
    x = x_ref[...]
    h = _norm_mod(x, g_ref[...], mod_ref[0, 3:4, :], mod_ref[0, 4:5, :]).astype(_bf16)
    zT = _dot_nt(winT_ref[...], h)
    q_lat = zT[_OFF[0]:_OFF[1]]
    kv_lat = zT[_OFF[1]:_OFF[2]]
    k_rope = zT[_OFF[2]:_OFF[3]]
    q_g = zT[_OFF[3]:_OFF[4]]
    k_g = zT[_OFF[4]:_OFF[5]]
    v_g = zT[_OFF[5]:_OFF[6]]
    t64 = t64_ref[...]
    t32 = t32_ref[...]

    zeros32 = jnp.zeros((HEAD_PAD - MLA_NOPE_DIM - MLA_ROPE_DIM, tm), _f32)
    zeros64 = jnp.zeros((HEAD_PAD - GQA_HEAD_DIM, tm), _f32)
    ones_pad = (lax.broadcasted_iota(jnp.int32, (V_ROWS - V_DIM, tm), 0) == 0).astype(_f32)

    mla_scale = (MLA_NOPE_DIM + MLA_ROPE_DIM) ** -0.5 * LOG2E
    qn = _rms_fm(q_lat, gq_ref[...], mla_scale).astype(_bf16)
    qa = _dot(wuqT_ref[...], qn)
    nope_rows = MLA_HEADS * MLA_NOPE_DIM
    for hd in range(MLA_HEADS):
        nope = qa[hd * MLA_NOPE_DIM:(hd + 1) * MLA_NOPE_DIM]
        pe = _rope_fm(qa[nope_rows + hd * MLA_ROPE_DIM:nope_rows + (hd + 1) * MLA_ROPE_DIM], t32)
        qT_ref[0, hd, 0] = jnp.concatenate([nope, pe, zeros32], axis=0).astype(_bf16)

    kvn = _rms_fm(kv_lat, gkv_ref[...]).astype(_bf16)
    kva = _dot(wukvT_ref[...], kvn)
    k_pe = _rope_fm(k_rope, t32)
    for hd in range(MLA_HEADS):
        k_nope = kva[hd * MLA_NOPE_DIM:(hd + 1) * MLA_NOPE_DIM]
        kT = jnp.concatenate([k_nope, k_pe, zeros32], axis=0)
        k_ref[0, hd] = kT.T.astype(_bf16)
        v = kva[nope_rows + hd * MLA_V_DIM:nope_rows + (hd + 1) * MLA_V_DIM]
        vT_ref[0, hd] = jnp.concatenate([v, ones_pad], axis=0).astype(_bf16)

    gqa_scale = GQA_HEAD_DIM ** -0.5 * LOG2E
    for hd in range(GQA_HEADS):
        xh = _rms_fm(q_g[hd * GQA_HEAD_DIM:(hd + 1) * GQA_HEAD_DIM], gqh_ref[...], gqa_scale)
        qT_ref[0, MLA_HEADS + hd, 0] = jnp.concatenate([_rope_fm(xh, t64), zeros64], axis=0).astype(_bf16)
    for hd in range(GQA_KV_HEADS):
        xh = _rms_fm(k_g[hd * GQA_HEAD_DIM:(hd + 1) * GQA_HEAD_DIM], gkh_ref[...])
        kT = jnp.concatenate([_rope_fm(xh, t64), zeros64], axis=0)
        k_ref[0, MLA_HEADS + hd] = kT.T.astype(_bf16)
        v = v_g[hd * GQA_HEAD_DIM:(hd + 1) * GQA_HEAD_DIM]
        vT_ref[0, MLA_HEADS + hd] = jnp.concatenate([v, ones_pad], axis=0).astype(_bf16)


def _mix_pre(x2d, mod3, g, winT, gq, wuqT, gkv, wukvT, gqh, gkh, t64, t32, bsz, seq):
    n, d = x2d.shape
    tm = TM_MIX
    tpb = seq // tm
    out_shape = (jax.ShapeDtypeStruct((bsz, N_Q_HEADS, tpb, HEAD_PAD, tm), _bf16),
                 jax.ShapeDtypeStruct((bsz, N_KV_HEADS, seq, HEAD_PAD), _bf16),
                 jax.ShapeDtypeStruct((bsz, N_KV_HEADS, V_ROWS, seq), _bf16))
    return pl.pallas_call(
        _mix_pre_kernel,
        grid=(n // tm,),
        in_specs=[pl.BlockSpec((tm, d), lambda i: (i, 0)),
                  pl.BlockSpec((1, N_MOD, d), lambda i: (i // tpb, 0, 0)),
                  _const_spec((1, d)),
                  _const_spec(winT.shape),
                  _const_spec(gq.shape),
                  _const_spec(wuqT.shape),
                  _const_spec(gkv.shape),
                  _const_spec(wukvT.shape),
                  _const_spec(gqh.shape),
                  _const_spec(gkh.shape),
                  pl.BlockSpec((GQA_HEAD_DIM, tm), lambda i: (0, i % tpb)),
                  pl.BlockSpec((MLA_ROPE_DIM, tm), lambda i: (0, i % tpb))],
        out_specs=(pl.BlockSpec((1, N_Q_HEADS, 1, HEAD_PAD, tm), lambda i: (i // tpb, 0, i % tpb, 0, 0)),
                   pl.BlockSpec((1, N_KV_HEADS, tm, HEAD_PAD), lambda i: (i // tpb, 0, i % tpb, 0)),
                   pl.BlockSpec((1, N_KV_HEADS, V_ROWS, tm), lambda i: (i // tpb, 0, 0, i % tpb))),
        out_shape=out_shape,
        compiler_params=pltpu.CompilerParams(vmem_limit_bytes=VMEM_LIMIT_BYTES),
        name="mix_pre",
    )(x2d, mod3, g, winT, gq, wuqT, gkv, wukvT, gqh, gkh, t64, t32)


def _attn_kernel(q_ref, qn_ref, k_ref, kn_ref, vT_ref, o_ref, s_even, s_odd, m_even, m_odd, acc_ref):
    n_tiles = q_ref.shape[2]
    seq, tq = s_even.shape
    n_chunks = seq // KV_CHUNK

    def score_chunk(q, kk_ref, s_w, c, m_run):
        rows = slice(c * KV_CHUNK, (c + 1) * KV_CHUNK)
        s = _dot(kk_ref[0, 0, rows, :], q)
        s_w[rows, :] = s
        mc = jnp.max(s.reshape(KV_CHUNK // 8, 8, tq), axis=0)
        return mc if m_run is None else jnp.maximum(m_run, mc)

    def reduce_chunk(s_r, m, c, acc):
        rows = slice(c * KV_CHUNK, (c + 1) * KV_CHUNK)
        p = jnp.exp2(s_r[rows, :] - m).astype(_bf16)
        part = _dot(vT_ref[0, 0, :, rows], p)
        return part if acc is None else acc + part

    s_bufs, m_bufs = (s_even, s_odd), (m_even, m_odd)

    def col_max(m_run):
        return jnp.max(m_run, axis=0, keepdims=True)

    def finish(acc, t):
        o_ref[0, 0, t] = acc[0:V_DIM] * (1.0 / acc[V_DIM:V_DIM + 1])

    @pl.when(pl.program_id(0) == 0)
    def _():
        m_run = None
        for c in range(n_chunks):
            m_run = score_chunk(q_ref[0, 0, 0], k_ref, s_even, c, m_run)
        m_even[...] = col_max(m_run)

    def run_tiles(j_lo, j_hi):
        m_cur = m_bufs[j_lo % 2][...]
        if j_lo >= 1:
            m_prev, acc_prev = m_bufs[(j_lo - 1) % 2][...], acc_ref[...]
        for j in range(j_lo, j_hi):
            if j + 1 < n_tiles:
                q, kk_ref = q_ref[0, 0, j + 1], k_ref
            else:
                q, kk_ref = qn_ref[0, 0, 0], kn_ref
            s_w, s_r = s_bufs[(j + 1) % 2], s_bufs[j % 2]
            m_run, acc = None, None
            for c in range(n_chunks):
                if c == 0 and j >= 1:
                    finish(reduce_chunk(s_w, m_prev, n_chunks - 1, acc_prev), j - 1)
                m_run = score_chunk(q, kk_ref, s_w, c, m_run)
                if c >= 1:
                    acc = reduce_chunk(s_r, m_cur, c - 1, acc)
            m_prev, m_cur, acc_prev = m_cur, col_max(m_run), acc
        if j_hi == n_tiles:
            finish(reduce_chunk(s_bufs[(j_hi - 1) % 2], m_prev, n_chunks - 1, acc_prev), j_hi - 1)
        else:
            m_bufs[(j_hi - 1) % 2][...] = m_prev
            acc_ref[...] = acc_prev
        m_bufs[j_hi % 2][...] = m_cur

    def first_half(_, carry):
        run_tiles(0, n_tiles // 2)
        return carry

    lax.fori_loop(0, 1 + (pl.program_id(0) >> 30), first_half, 0)
    run_tiles(n_tiles // 2, n_tiles)


def _kv_head(h):
    return jnp.where(h < MLA_HEADS, h, MLA_HEADS + (h - MLA_HEADS) // GQA_GROUP)


def _attention(qT, k, vT):
    bsz, _, n_tiles, _, tq = qT.shape
    seq = n_tiles * tq
    assert n_tiles % 2 == 0
    n_steps = bsz * N_Q_HEADS

    def cur(g):
        return g // N_Q_HEADS, g % N_Q_HEADS

    def nxt(g):
        return cur(jnp.minimum(g + 1, n_steps - 1))

    def q_map(bh):
        return lambda g: (*bh(g), 0, 0, 0)

    def kv_map(bh):
        def index(g):
            b, h = bh(g)
            return b, _kv_head(h), 0, 0
        return index

    return pl.pallas_call(
        _attn_kernel,
        grid=(n_steps,),
        in_specs=[pl.BlockSpec((1, 1, n_tiles, HEAD_PAD, tq), q_map(cur)),
                  pl.BlockSpec((1, 1, 1, HEAD_PAD, tq), q_map(nxt)),
                  pl.BlockSpec((1, 1, seq, HEAD_PAD), kv_map(cur)),
                  pl.BlockSpec((1, 1, seq, HEAD_PAD), kv_map(nxt)),
                  pl.BlockSpec((1, 1, V_ROWS, seq), kv_map(cur))],
        out_specs=pl.BlockSpec((1, 1, n_tiles, V_DIM, tq), q_map(cur)),
        out_shape=jax.ShapeDtypeStruct((bsz, N_Q_HEADS, n_tiles, V_DIM, tq), _f32),
        scratch_shapes=[pltpu.VMEM((seq, tq), _f32), pltpu.VMEM((seq, tq), _f32),
                        pltpu.VMEM((1, tq), _f32), pltpu.VMEM((1, tq), _f32),
                        pltpu.VMEM((V_ROWS, tq), _f32)],
        compiler_params=pltpu.CompilerParams(dimension_semantics=("arbitrary",),
                                             vmem_limit_bytes=VMEM_LIMIT_BYTES),
        name="attn",
    )(qT, qT, k, k, vT)


def _post_kernel(x_ref, o_ref_in, mod_ref, gom_ref, gog_ref, wout_ref, g2_ref, wgu_ref, wd_ref, gf_ref, out_ref,
                 *, final_norm):
    tm = x_ref.shape[0]
    x = x_ref[...]
    oT = o_ref_in[0, :, 0].reshape(N_Q_HEADS * V_DIM, tm)
    half = MLA_HEADS * MLA_V_DIM
    on = jnp.concatenate([_rms_fm(oT[:half], gom_ref[...]), _rms_fm(oT[half:], gog_ref[...])], axis=0)
    y = _dot_tn(on.astype(_bf16), wout_ref[...])
    x = x + mod_ref[0, 5:6, :] * y
    h = _norm_mod(x, g2_ref[...], mod_ref[0, 6:7, :], mod_ref[0, 7:8, :]).astype(_bf16)
    x = x + (0.5 * mod_ref[0, 8:9, :]) * _swiglu(h, wgu_ref, wd_ref)
    if final_norm:
        r = lax.rsqrt(jnp.mean(x * x, axis=-1, keepdims=True) + EPS)
        x = x * r * gf_ref[...]
    out_ref[...] = x


def _post(x2d, oT, mod3, gom, gog, wout, g2, wgu, wd, gf, seq, final_norm):
    n, d = x2d.shape
    tm = oT.shape[-1]
    tpb = seq // tm
    return pl.pallas_call(
        functools.partial(_post_kernel, final_norm=final_norm),
        grid=(n // tm,),
        in_specs=[pl.BlockSpec((tm, d), lambda i: (i, 0)),
                  pl.BlockSpec((1, N_Q_HEADS, 1, V_DIM, tm), lambda i: (i // tpb, 0, i % tpb, 0, 0)),
                  pl.BlockSpec((1, N_MOD, d), lambda i: (i // tpb, 0, 0)),
                  _const_spec(gom.shape),
                  _const_spec(gog.shape),
                  _const_spec(wout.shape),
                  _const_spec((1, d)),
                  _const_spec(wgu.shape),
                  _const_spec(wd.shape),
                  _const_spec((1, d))],
        out_specs=pl.BlockSpec((tm, d), lambda i: (i, 0)),
        out_shape=jax.ShapeDtypeStruct((n, d), _f32),
        compiler_params=pltpu.CompilerParams(vmem_limit_bytes=VMEM_LIMIT_BYTES),
        name="post",
    )(x2d, oT, mod3, gom, gog, wout, g2, wgu, wd, gf)


def _rope_table(seq, dim):
    rows = seq // GRID_W
    row = jnp.repeat(jnp.arange(rows), GRID_W).astype(_f32)
    col = jnp.tile(jnp.arange(GRID_W), rows).astype(_f32)
    axis_dim = dim // 2
    inv_freq = ROPE_THETA ** (-(jnp.arange(axis_dim // 2, dtype=_f32) * 2.0 / axis_dim))
    ang_row = row[:, None] * inv_freq[None, :]
    ang_col = col[:, None] * inv_freq[None, :]
    return jnp.concatenate([jnp.cos(ang_row), jnp.sin(ang_row), jnp.cos(ang_col), jnp.sin(ang_col)], axis=1).T


def kernel(x, c, w_ada, b_ada, g_ffn1, w1_gu, w1_down, g_mix, w_in, g_q_lat, w_uq, g_kv_lat, w_ukv,
           g_qhead, g_khead, g_out_mla, g_out_gqa, w_out, g_ffn2, w2_gu, w2_down, g_final):
    bsz, seq, d = x.shape
    depth = w_ada.shape[0]
    t64 = _rope_table(seq, GQA_HEAD_DIM)
    t32 = _rope_table(seq, MLA_ROPE_DIM)
    x2d = x.reshape(bsz * seq, d)
    for l in range(depth):
        mod3 = _ada(c, w_ada[l], b_ada[l]).reshape(bsz, N_MOD, d)

        x2d = _ffn(x2d, mod3, g_ffn1[l].reshape(1, d), w1_gu[l].astype(_bf16), w1_down[l].astype(_bf16), seq, 0)

        winT = w_in[l].T.astype(_bf16)
        wuq = w_uq[l].reshape(Q_LORA_RANK, MLA_HEADS, MLA_NOPE_DIM + MLA_ROPE_DIM)
        wuqT = jnp.concatenate([wuq[:, :, :MLA_NOPE_DIM].reshape(Q_LORA_RANK, -1),
                                wuq[:, :, MLA_NOPE_DIM:].reshape(Q_LORA_RANK, -1)], axis=1).T.astype(_bf16)
        wukv = w_ukv[l].reshape(KV_LORA_RANK, MLA_HEADS, MLA_NOPE_DIM + MLA_V_DIM)
        wukvT = jnp.concatenate([wukv[:, :, :MLA_NOPE_DIM].reshape(KV_LORA_RANK, -1),
                                 wukv[:, :, MLA_NOPE_DIM:].reshape(KV_LORA_RANK, -1)], axis=1).T.astype(_bf16)
        qT, k, vT = _mix_pre(x2d, mod3, g_mix[l].reshape(1, d), winT,
                             g_q_lat[l].reshape(-1, 1), wuqT, g_kv_lat[l].reshape(-1, 1), wukvT,
                             g_qhead[l].reshape(-1, 1), g_khead[l].reshape(-1, 1), t64, t32, bsz, seq)

        oT = _attention(qT, k, vT)

        x2d = _post(x2d, oT, mod3, g_out_mla[l].reshape(-1, 1), g_out_gqa[l].reshape(-1, 1),
                    w_out[l].astype(_bf16), g_ffn2[l].reshape(1, d),
                    w2_gu[l].astype(_bf16), w2_down[l].astype(_bf16), g_final.reshape(1, d), seq,
                    final_norm=(l == depth - 1))
    return x2d.reshape(bsz, seq, d)
```

```python
import functools
import math

import jax
import jax.numpy as jnp
import numpy as np
from jax import lax
from jax.experimental import pallas as pl
from jax.experimental.pallas import tpu as pltpu

D_MODEL = 1024
GRID_W = 64
ROPE_THETA = 10000.0
EPS = 1e-6

MLA_HEADS = 8
MLA_NOPE_DIM = 64
MLA_ROPE_DIM = 32
MLA_V_DIM = 64
Q_LORA_RANK = 256
KV_LORA_RANK = 128

GQA_HEADS = 8
GQA_KV_HEADS = 2
GQA_HEAD_DIM = 64
GQA_GROUP = GQA_HEADS // GQA_KV_HEADS

D_FF = 2816
N_MOD = 9

N_Q_HEADS = MLA_HEADS + GQA_HEADS
N_KV_HEADS = MLA_HEADS + GQA_KV_HEADS
HEAD_PAD = 128
V_DIM = 64
BF16_SUBLANES = 16
V_ROWS = V_DIM + BF16_SUBLANES
LOG2E = math.log2(math.e)

_OFF = np.cumsum([0, Q_LORA_RANK, KV_LORA_RANK, MLA_ROPE_DIM,
                  GQA_HEADS * GQA_HEAD_DIM, GQA_KV_HEADS * GQA_HEAD_DIM, GQA_KV_HEADS * GQA_HEAD_DIM])
D_IN = int(_OFF[-1])

VMEM_LIMIT_BYTES = 56 * 1024 * 1024

TM_FFN = 1024
TQ = 512
TM_MIX = 1024
MXU_TILE = 256
FF_CHUNK_TILES = (6, 5)
KV_CHUNK = 256

_bf16 = jnp.bfloat16
_f32 = jnp.float32


def _dot(a, b):
    return jnp.dot(a, b, preferred_element_type=_f32)


def _dot_nt(a, b):
    return lax.dot_general(a, b, (((1,), (1,)), ((), ())), preferred_element_type=_f32)


def _dot_tn(a, b):
    return lax.dot_general(a, b, (((0,), (0,)), ((), ())), preferred_element_type=_f32)


def _const_spec(shape):
    return pl.BlockSpec(shape, lambda *_: (0,) * len(shape), pipeline_mode=pl.Buffered(1))


def _ada_kernel(c_ref, w_ref, b_ref, o_ref):
    c = c_ref[...]
    ca = c * jax.nn.sigmoid(c)
    w = w_ref[...]
    ca_hi = ca.astype(_bf16)
    ca_lo = (ca - ca_hi.astype(_f32)).astype(_bf16)
    w_hi = w.astype(_bf16)
    w_lo = (w - w_hi.astype(_f32)).astype(_bf16)
    o_ref[...] = _dot(ca_hi, w_hi) + (_dot(ca_hi, w_lo) + _dot(ca_lo, w_hi)) + b_ref[...]


def _ada(c, w_ada, b_ada):
    bsz, d = c.shape
    n = w_ada.shape[1]
    tn = 1152
    return pl.pallas_call(
        _ada_kernel,
        grid=(n // tn,),
        in_specs=[pl.BlockSpec((bsz, d), lambda j: (0, 0)),
                  pl.BlockSpec((d, tn), lambda j: (0, j)),
                  pl.BlockSpec((1, tn), lambda j: (0, j))],
        out_specs=pl.BlockSpec((bsz, tn), lambda j: (0, j)),
        out_shape=jax.ShapeDtypeStruct((bsz, n), _f32),
        compiler_params=pltpu.CompilerParams(vmem_limit_bytes=VMEM_LIMIT_BYTES),
        name="ada",
    )(c, w_ada, b_ada.reshape(1, n))


def _norm_mod(x, g, shift, scale):
    r = lax.rsqrt(jnp.mean(x * x, axis=-1, keepdims=True) + EPS)
    return (x * r) * (g * (1.0 + scale)) + shift


def _swiglu(h, wgu_ref, wd_ref):
    acc = None
    lo = 0
    for tiles in FF_CHUNK_TILES:
        hi = lo + tiles * MXU_TILE
        a = _dot(h, wgu_ref[:, lo:hi])
        b = _dot(h, wgu_ref[:, D_FF + lo:D_FF + hi])
        act = (a * jax.nn.sigmoid(a) * b).astype(_bf16)
        part = _dot(act, wd_ref[lo:hi, :])
        acc = part if acc is None else acc + part
        lo = hi
    assert lo == D_FF
    return acc


def _ffn_kernel(x_ref, mod_ref, g_ref, wgu_ref, wd_ref, wa_ref, wb_ref, o_ref, wa_out, wb_out, *, mod_base):
    shift = mod_ref[0, mod_base:mod_base + 1, :]
    scale = mod_ref[0, mod_base + 1:mod_base + 2, :]
    gate = mod_ref[0, mod_base + 2:mod_base + 3, :]
    x = x_ref[...]
    h = _norm_mod(x, g_ref[...], shift, scale).astype(_bf16)
    o_ref[...] = x + (0.5 * gate) * _swiglu(h, wgu_ref, wd_ref)
    wa_out[...] = wa_ref[...].astype(_bf16)
    wb_out[...] = wb_ref[...].astype(_bf16)


def _ffn(x2d, mod3, g, wgu, wd, seq, mod_base, cast_a, cast_b):
    n, d = x2d.shape
    tm = TM_FFN
    tpb = seq // tm
    n_steps = n // tm

    def cast_spec(w):
        rows = w.shape[0]
        visits = 1
        while rows % (n_steps // visits) or (rows // (n_steps // visits)) % BF16_SUBLANES:
            visits *= 2
            assert visits <= n_steps
        return pl.BlockSpec((rows // (n_steps // visits), w.shape[1]), lambda i: (i // visits, 0))

    return pl.pallas_call(
        functools.partial(_ffn_kernel, mod_base=mod_base),
        grid=(n_steps,),
        in_specs=[pl.BlockSpec((tm, d), lambda i: (i, 0)),
                  pl.BlockSpec((1, N_MOD, d), lambda i: (i // tpb, 0, 0)),
                  _const_spec((1, d)),
                  _const_spec(wgu.shape),
                  _const_spec(wd.shape),
                  cast_spec(cast_a),
                  cast_spec(cast_b)],
        out_specs=(pl.BlockSpec((tm, d), lambda i: (i, 0)), cast_spec(cast_a), cast_spec(cast_b)),
        out_shape=(jax.ShapeDtypeStruct((n, d), _f32),
                   jax.ShapeDtypeStruct(cast_a.shape, _bf16),
                   jax.ShapeDtypeStruct(cast_b.shape, _bf16)),
        compiler_params=pltpu.CompilerParams(vmem_limit_bytes=VMEM_LIMIT_BYTES),
        name="ffn1",
    )(x2d, mod3, g, wgu, wd, cast_a, cast_b)


def _rope_fm(x, tab):
    q = x.shape[0] // 4
    x1r, x2r, x1c, x2c = x[0:q], x[q:2 * q], x[2 * q:3 * q], x[3 * q:4 * q]
    cr, sr, cc, sc = tab[0:q], tab[q:2 * q], tab[2 * q:3 * q], tab[3 * q:4 * q]
    return jnp.concatenate([x1r * cr - x2r * sr, x1r * sr + x2r * cr,
                            x1c * cc - x2c * sc, x1c * sc + x2c * cc], axis=0)


def _rms_fm(x, g_col, extra=1.0):
    r = lax.rsqrt(jnp.mean(x * x, axis=0, keepdims=True) + EPS)
    if extra != 1.0:
        r = r * extra
    return x * r * g_col


def _mix_pre_kernel(x_ref, mod_ref, g_ref, winT_ref, gq_ref, wuqT_ref, gkv_ref, wukvT_ref,
                    gqh_ref, gkh_ref, t64_ref, t32_ref, qT_ref, k_ref, vT_ref):
    tm = x_ref.shape[0]
    x = x_ref[...]
    h = _norm_mod(x, g_ref[...], mod_ref[0, 3:4, :], mod_ref[0, 4:5, :]).astype(_bf16)
    zT = _dot_nt(winT_ref[...], h)
    q_lat = zT[_OFF[0]:_OFF[1]]
    kv_lat = zT[_OFF[1]:_OFF[2]]
    k_rope = zT[_OFF[2]:_OFF[3]]
    q_g = zT[_OFF[3]:_OFF[4]]
    k_g = zT[_OFF[4]:_OFF[5]]
    v_g = zT[_OFF[5]:_OFF[6]]
    t64 = t64_ref[...]
    t32 = t32_ref[...]

    zeros32 = jnp.zeros((HEAD_PAD - MLA_NOPE_DIM - MLA_ROPE_DIM, tm), _f32)
    zeros64 = jnp.zeros((HEAD_PAD - GQA_HEAD_DIM, tm), _f32)
    ones_pad = (lax.broadcasted_iota(jnp.int32, (V_ROWS - V_DIM, tm), 0) == 0).astype(_f32)

    def put_q(head, qh):
        for t in range(tm // TQ):
            qT_ref[0, head, t] = qh[:, t * TQ:(t + 1) * TQ].astype(_bf16)

    mla_scale = (MLA_NOPE_DIM + MLA_ROPE_DIM) ** -0.5 * LOG2E
    qn = _rms_fm(q_lat, gq_ref[...], mla_scale).astype(_bf16)
    qa = _dot(wuqT_ref[...], qn)
    nope_rows = MLA_HEADS * MLA_NOPE_DIM
    for hd in range(MLA_HEADS):
        nope = qa[hd * MLA_NOPE_DIM:(hd + 1) * MLA_NOPE_DIM]
        pe = _rope_fm(qa[nope_rows + hd * MLA_ROPE_DIM:nope_rows + (hd + 1) * MLA_ROPE_DIM], t32)
        put_q(hd, jnp.concatenate([nope, pe, zeros32], axis=0))

    kvn = _rms_fm(kv_lat, gkv_ref[...]).astype(_bf16)
    kva = _dot(wukvT_ref[...], kvn)
    k_pe = _rope_fm(k_rope, t32)
    for hd in range(MLA_HEADS):
        k_nope = kva[hd * MLA_NOPE_DIM:(hd + 1) * MLA_NOPE_DIM]
        kT = jnp.concatenate([k_nope, k_pe, zeros32], axis=0)
        k_ref[0, hd] = kT.T.astype(_bf16)
        v = kva[nope_rows + hd * MLA_V_DIM:nope_rows + (hd + 1) * MLA_V_DIM]
        vT_ref[0, hd] = jnp.concatenate([v, ones_pad], axis=0).astype(_bf16)

    gqa_scale = GQA_HEAD_DIM ** -0.5 * LOG2E
    for hd in range(GQA_HEADS):
        xh = _rms_fm(q_g[hd * GQA_HEAD_DIM:(hd + 1) * GQA_HEAD_DIM], gqh_ref[...], gqa_scale)
        put_q(MLA_HEADS + hd, jnp.concatenate([_rope_fm(xh, t64), zeros64], axis=0))
    for hd in range(GQA_KV_HEADS):
        xh = _rms_fm(k_g[hd * GQA_HEAD_DIM:(hd + 1) * GQA_HEAD_DIM], gkh_ref[...])
        kT = jnp.concatenate([_rope_fm(xh, t64), zeros64], axis=0)
        k_ref[0, MLA_HEADS + hd] = kT.T.astype(_bf16)
        v = v_g[hd * GQA_HEAD_DIM:(hd + 1) * GQA_HEAD_DIM]
        vT_ref[0, MLA_HEADS + hd] = jnp.concatenate([v, ones_pad], axis=0).astype(_bf16)


def _mix_pre(x2d, mod3, g, winT, gq, wuqT, gkv, wukvT, gqh, gkh, t64, t32, bsz, seq):
    n, d = x2d.shape
    tm = TM_MIX
    tpb = seq // tm
    qpt = tm // TQ
    out_shape = (jax.ShapeDtypeStruct((bsz, N_Q_HEADS, seq // TQ, HEAD_PAD, TQ), _bf16),
                 jax.ShapeDtypeStruct((bsz, N_KV_HEADS, seq, HEAD_PAD), _bf16),
                 jax.ShapeDtypeStruct((bsz, N_KV_HEADS, V_ROWS, seq), _bf16))
    return pl.pallas_call(
        _mix_pre_kernel,
        grid=(n // tm,),
        in_specs=[pl.BlockSpec((tm, d), lambda i: (i, 0)),
                  pl.BlockSpec((1, N_MOD, d), lambda i: (i // tpb, 0, 0)),
                  _const_spec((1, d)),
                  _const_spec(winT.shape),
                  _const_spec(gq.shape),
                  _const_spec(wuqT.shape),
                  _const_spec(gkv.shape),
                  _const_spec(wukvT.shape),
                  _const_spec(gqh.shape),
                  _const_spec(gkh.shape),
                  pl.BlockSpec((GQA_HEAD_DIM, tm), lambda i: (0, i % tpb)),
                  pl.BlockSpec((MLA_ROPE_DIM, tm), lambda i: (0, i % tpb))],
        out_specs=(pl.BlockSpec((1, N_Q_HEADS, qpt, HEAD_PAD, TQ), lambda i: (i // tpb, 0, i % tpb, 0, 0)),
                   pl.BlockSpec((1, N_KV_HEADS, tm, HEAD_PAD), lambda i: (i // tpb, 0, i % tpb, 0)),
                   pl.BlockSpec((1, N_KV_HEADS, V_ROWS, tm), lambda i: (i // tpb, 0, 0, i % tpb))),
        out_shape=out_shape,
        compiler_params=pltpu.CompilerParams(vmem_limit_bytes=VMEM_LIMIT_BYTES),
        name="mix_pre",
    )(x2d, mod3, g, winT, gq, wuqT, gkv, wukvT, gqh, gkh, t64, t32)


def _attn_kernel(q_ref, qn_ref, k_ref, kn_ref, vT_ref, o_ref, s_even, s_odd, m_even, m_odd, acc_ref):
    n_tiles = q_ref.shape[2]
    seq, tq = s_even.shape
    n_chunks = seq // KV_CHUNK

    def score_chunk(q, kk_ref, s_w, c, m_run):
        rows = slice(c * KV_CHUNK, (c + 1) * KV_CHUNK)
        s = _dot(kk_ref[0, 0, rows, :], q)
        s_w[rows, :] = s
        mc = jnp.max(s.reshape(KV_CHUNK // 8, 8, tq), axis=0)
        return mc if m_run is None else jnp.maximum(m_run, mc)

    def reduce_chunk(s_r, m, c, acc):
        rows = slice(c * KV_CHUNK, (c + 1) * KV_CHUNK)
        p = jnp.exp2(s_r[rows, :] - m).astype(_bf16)
        part = _dot(vT_ref[0, 0, :, rows], p)
        return part if acc is None else acc + part

    s_bufs, m_bufs = (s_even, s_odd), (m_even, m_odd)

    def col_max(m_run):
        return jnp.max(m_run, axis=0, keepdims=True)

    def finish(acc, t):
        o_ref[0, 0, t] = acc[0:V_DIM] * (1.0 / acc[V_DIM:V_DIM + 1])

    @pl.when(pl.program_id(0) == 0)
    def _():
        m_run = None
        for c in range(n_chunks):
            m_run = score_chunk(q_ref[0, 0, 0], k_ref, s_even, c, m_run)
        m_even[...] = col_max(m_run)

    def run_tiles(j_lo, j_hi):
        m_cur = m_bufs[j_lo % 2][...]
        if j_lo >= 1:
            m_prev, acc_prev = m_bufs[(j_lo - 1) % 2][...], acc_ref[...]
        for j in range(j_lo, j_hi):
            if j + 1 < n_tiles:
                q, kk_ref = q_ref[0, 0, j + 1], k_ref
            else:
                q, kk_ref = qn_ref[0, 0, 0], kn_ref
            s_w, s_r = s_bufs[(j + 1) % 2], s_bufs[j % 2]
            m_run, acc = None, None
            for c in range(n_chunks):
                if c == 0 and j >= 1:
                    finish(reduce_chunk(s_w, m_prev, n_chunks - 1, acc_prev), j - 1)
                m_run = score_chunk(q, kk_ref, s_w, c, m_run)
                if c >= 1:
                    acc = reduce_chunk(s_r, m_cur, c - 1, acc)
            m_prev, m_cur, acc_prev = m_cur, col_max(m_run), acc
        if j_hi == n_tiles:
            finish(reduce_chunk(s_bufs[(j_hi - 1) % 2], m_prev, n_chunks - 1, acc_prev), j_hi - 1)
        else:
            m_bufs[(j_hi - 1) % 2][...] = m_prev
            acc_ref[...] = acc_prev
        m_bufs[j_hi % 2][...] = m_cur

    def first_half(_, carry):
        run_tiles(0, n_tiles // 2)
        return carry

    lax.fori_loop(0, 1 + (pl.program_id(0) >> 30), first_half, 0)
    run_tiles(n_tiles // 2, n_tiles)


def _kv_head(h):
    return jnp.where(h < MLA_HEADS, h, MLA_HEADS + (h - MLA_HEADS) // GQA_GROUP)


def _attention(qT, k, vT):
    bsz, _, n_tiles, _, tq = qT.shape
    seq = n_tiles * tq
    assert n_tiles % 2 == 0
    n_steps = bsz * N_Q_HEADS

    def cur(g):
        return g // N_Q_HEADS, g % N_Q_HEADS

    def nxt(g):
        return cur(jnp.minimum(g + 1, n_steps - 1))

    def q_map(bh):
        return lambda g: (*bh(g), 0, 0, 0)

    def kv_map(bh):
        def index(g):
            b, h = bh(g)
            return b, _kv_head(h), 0, 0
        return index

    return pl.pallas_call(
        _attn_kernel,
        grid=(n_steps,),
        in_specs=[pl.BlockSpec((1, 1, n_tiles, HEAD_PAD, tq), q_map(cur)),
                  pl.BlockSpec((1, 1, 1, HEAD_PAD, tq), q_map(nxt)),
                  pl.BlockSpec((1, 1, seq, HEAD_PAD), kv_map(cur)),
                  pl.BlockSpec((1, 1, seq, HEAD_PAD), kv_map(nxt)),
                  pl.BlockSpec((1, 1, V_ROWS, seq), kv_map(cur))],
        out_specs=pl.BlockSpec((1, 1, n_tiles, V_DIM, tq), q_map(cur)),
        out_shape=jax.ShapeDtypeStruct((bsz, N_Q_HEADS, n_tiles, V_DIM, tq), _f32),
        scratch_shapes=[pltpu.VMEM((seq, tq), _f32), pltpu.VMEM((seq, tq), _f32),
                        pltpu.VMEM((1, tq), _f32), pltpu.VMEM((1, tq), _f32),
                        pltpu.VMEM((V_ROWS, tq), _f32)],
        compiler_params=pltpu.CompilerParams(dimension_semantics=("arbitrary",),
                                             vmem_limit_bytes=VMEM_LIMIT_BYTES),
        name="attn",
    )(qT, qT, k, k, vT)


def _post_kernel(x_ref, o_ref_in, mod_ref, gom_ref, gog_ref, wout_ref, g2_ref, wgu_ref, wd_ref, gf_ref, out_ref,
                 *, final_norm):
    tm = x_ref.shape[0]
    x = x_ref[...]
    oT = o_ref_in[0, :, 0].reshape(N_Q_HEADS * V_DIM, tm)
    half = MLA_HEADS * MLA_V_DIM
    on = jnp.concatenate([_rms_fm(oT[:half], gom_ref[...]), _rms_fm(oT[half:], gog_ref[...])], axis=0)
    y = _dot_tn(on.astype(_bf16), wout_ref[...])
    x = x + mod_ref[0, 5:6, :] * y
    h = _norm_mod(x, g2_ref[...], mod_ref[0, 6:7, :], mod_ref[0, 7:8, :]).astype(_bf16)
    x = x + (0.5 * mod_ref[0, 8:9, :]) * _swiglu(h, wgu_ref, wd_ref)
    if final_norm:
        r = lax.rsqrt(jnp.mean(x * x, axis=-1, keepdims=True) + EPS)
        x = x * r * gf_ref[...]
    out_ref[...] = x


def _post(x2d, oT, mod3, gom, gog, wout, g2, wgu, wd, gf, seq, final_norm):
    n, d = x2d.shape
    tm = oT.shape[-1]
    tpb = seq // tm
    return pl.pallas_call(
        functools.partial(_post_kernel, final_norm=final_norm),
        grid=(n // tm,),
        in_specs=[pl.BlockSpec((tm, d), lambda i: (i, 0)),
                  pl.BlockSpec((1, N_Q_HEADS, 1, V_DIM, tm), lambda i: (i // tpb, 0, i % tpb, 0, 0)),
                  pl.BlockSpec((1, N_MOD, d), lambda i: (i // tpb, 0, 0)),
                  _const_spec(gom.shape),
                  _const_spec(gog.shape),
                  _const_spec(wout.shape),
                  _const_spec((1, d)),
                  _const_spec(wgu.shape),
                  _const_spec(wd.shape),
                  _const_spec((1, d))],
        out_specs=pl.BlockSpec((tm, d), lambda i: (i, 0)),
        out_shape=jax.ShapeDtypeStruct((n, d), _f32),
        compiler_params=pltpu.CompilerParams(vmem_limit_bytes=VMEM_LIMIT_BYTES),
        name="post",
    )(x2d, oT, mod3, gom, gog, wout, g2, wgu, wd, gf)


def _rope_table(seq, dim):
    rows = seq // GRID_W
    row = jnp.repeat(jnp.arange(rows), GRID_W).astype(_f32)
    col = jnp.tile(jnp.arange(GRID_W), rows).astype(_f32)
    axis_dim = dim // 2
    inv_freq = ROPE_THETA ** (-(jnp.arange(axis_dim // 2, dtype=_f32) * 2.0 / axis_dim))
    ang_row = row[:, None] * inv_freq[None, :]
    ang_col = col[:, None] * inv_freq[None, :]
    return jnp.concatenate([jnp.cos(ang_row), jnp.sin(ang_row), jnp.cos(ang_col), jnp.sin(ang_col)], axis=1).T


def kernel(x, c, w_ada, b_ada, g_ffn1, w1_gu, w1_down, g_mix, w_in, g_q_lat, w_uq, g_kv_lat, w_ukv,
           g_qhead, g_khead, g_out_mla, g_out_gqa, w_out, g_ffn2, w2_gu, w2_down, g_final):
    bsz, seq, d = x.shape
    depth = w_ada.shape[0]
    t64 = _rope_table(seq, GQA_HEAD_DIM)
    t32 = _rope_table(seq, MLA_ROPE_DIM)
    x2d = x.reshape(bsz * seq, d)
    for l in range(depth):
        mod3 = _ada(c, w_ada[l], b_ada[l]).reshape(bsz, N_MOD, d)

        x2d, w2_gu_bf, w2_down_bf = _ffn(x2d, mod3, g_ffn1[l].reshape(1, d), w1_gu[l].astype(_bf16),
                                         w1_down[l].astype(_bf16), seq, 0, w2_gu[l], w2_down[l])

        winT = w_in[l].T.astype(_bf16)
        wuq = w_uq[l].reshape(Q_LORA_RANK, MLA_HEADS, MLA_NOPE_DIM + MLA_ROPE_DIM)
        wuqT = jnp.concatenate([wuq[:, :, :MLA_NOPE_DIM].reshape(Q_LORA_RANK, -1),
                                wuq[:, :, MLA_NOPE_DIM:].reshape(Q_LORA_RANK, -1)], axis=1).T.astype(_bf16)
        wukv = w_ukv[l].reshape(KV_LORA_RANK, MLA_HEADS, MLA_NOPE_DIM + MLA_V_DIM)
        wukvT = jnp.concatenate([wukv[:, :, :MLA_NOPE_DIM].reshape(KV_LORA_RANK, -1),
                                 wukv[:, :, MLA_NOPE_DIM:].reshape(KV_LORA_RANK, -1)], axis=1).T.astype(_bf16)
        qT, k, vT = _mix_pre(x2d, mod3, g_mix[l].reshape(1, d), winT,
                             g_q_lat[l].reshape(-1, 1), wuqT, g_kv_lat[l].reshape(-1, 1), wukvT,
                             g_qhead[l].reshape(-1, 1), g_khead[l].reshape(-1, 1), t64, t32, bsz, seq)

        oT = _attention(qT, k, vT)

        x2d = _post(x2d, oT, mod3, g_out_mla[l].reshape(-1, 1), g_out_gqa[l].reshape(-1, 1),
                    w_out[l].astype(_bf16), g_ffn2[l].reshape(1, d),
                    w2_gu_bf, w2_down_bf, g_final.reshape(1, d), seq,
                    final_norm=(l == depth - 1))
    return x2d.reshape(bsz, seq, d)
```

```python
import functools
import math

import jax
import jax.numpy as jnp
import numpy as np
from jax import lax
from jax.experimental import pallas as pl
from jax.experimental.pallas import tpu as pltpu

D_MODEL = 1024
GRID_W = 64
ROPE_THETA = 10000.0
EPS = 1e-6

MLA_HEADS = 8
MLA_NOPE_DIM = 64
MLA_ROPE_DIM = 32
MLA_V_DIM = 64
Q_LORA_RANK = 256
KV_LORA_RANK = 128

GQA_HEADS = 8
GQA_KV_HEADS = 2
GQA_HEAD_DIM = 64
GQA_GROUP = GQA_HEADS // GQA_KV_HEADS

D_FF = 2816
N_MOD = 9

N_Q_HEADS = MLA_HEADS + GQA_HEADS
N_KV_HEADS = MLA_HEADS + GQA_KV_HEADS
HEAD_PAD = 128
V_DIM = 64
F32_SUBLANES = 8
BF16_SUBLANES = 16
V_ROWS = V_DIM + BF16_SUBLANES
LOG2E = math.log2(math.e)

_OFF = np.cumsum([0, Q_LORA_RANK, KV_LORA_RANK, MLA_ROPE_DIM,
                  GQA_HEADS * GQA_HEAD_DIM, GQA_KV_HEADS * GQA_HEAD_DIM, GQA_KV_HEADS * GQA_HEAD_DIM])
D_IN = int(_OFF[-1])

VMEM_LIMIT_BYTES = 56 * 1024 * 1024

ADA_COLS = 1152
TM_FFN = 1024
TQ = 512
TM_MIX = 1024
MXU_TILE = 256
FF_CHUNK_TILES = (6, 5)
KV_CHUNK = MXU_TILE

_bf16 = jnp.bfloat16
_f32 = jnp.float32


def _dot(a, b):
    return jnp.dot(a, b, preferred_element_type=_f32)


def _dot_nt(a, b):
    return lax.dot_general(a, b, (((1,), (1,)), ((), ())), preferred_element_type=_f32)


def _dot_tn(a, b):
    return lax.dot_general(a, b, (((0,), (0,)), ((), ())), preferred_element_type=_f32)


def _const_spec(shape):
    return pl.BlockSpec(shape, lambda *_: (0,) * len(shape), pipeline_mode=pl.Buffered(1))


def _ada_kernel(c_ref, w_ref, b_ref, o_ref):
    c = c_ref[...]
    ca = c * jax.nn.sigmoid(c)
    w = w_ref[...]
    ca_hi = ca.astype(_bf16)
    ca_lo = (ca - ca_hi.astype(_f32)).astype(_bf16)
    w_hi = w.astype(_bf16)
    w_lo = (w - w_hi.astype(_f32)).astype(_bf16)
    o_ref[...] = _dot(ca_hi, w_hi) + (_dot(ca_hi, w_lo) + _dot(ca_lo, w_hi)) + b_ref[...]


def _ada(c, w_ada, b_ada):
    bsz, d = c.shape
    n = w_ada.shape[1]
    tn = ADA_COLS
    assert n % tn == 0
    return pl.pallas_call(
        _ada_kernel,
        grid=(n // tn,),
        in_specs=[pl.BlockSpec((bsz, d), lambda j: (0, 0)),
                  pl.BlockSpec((d, tn), lambda j: (0, j)),
                  pl.BlockSpec((1, tn), lambda j: (0, j))],
        out_specs=pl.BlockSpec((bsz, tn), lambda j: (0, j)),
        out_shape=jax.ShapeDtypeStruct((bsz, n), _f32),
        compiler_params=pltpu.CompilerParams(vmem_limit_bytes=VMEM_LIMIT_BYTES),
        name="ada",
    )(c, w_ada, b_ada.reshape(1, n))


def _norm_mod(x, g, shift, scale):
    r = lax.rsqrt(jnp.mean(x * x, axis=-1, keepdims=True) + EPS)
    return (x * r) * (g * (1.0 + scale)) + shift


def _swiglu(h, wgu_ref, wd_ref):
    acc = None
    lo = 0
    for tiles in FF_CHUNK_TILES:
        hi = lo + tiles * MXU_TILE
        a = _dot(h, wgu_ref[:, lo:hi])
        b = _dot(h, wgu_ref[:, D_FF + lo:D_FF + hi])
        act = (a * jax.nn.sigmoid(a) * b).astype(_bf16)
        part = _dot(act, wd_ref[lo:hi, :])
        acc = part if acc is None else acc + part
        lo = hi
    assert lo == D_FF
    return acc


def _ffn_kernel(x_ref, mod_ref, g_ref, wgu_ref, wd_ref, wa_ref, wb_ref, o_ref, wa_out, wb_out, *, mod_base):
    shift = mod_ref[0, mod_base:mod_base + 1, :]
    scale = mod_ref[0, mod_base + 1:mod_base + 2, :]
    gate = mod_ref[0, mod_base + 2:mod_base + 3, :]
    x = x_ref[...]
    h = _norm_mod(x, g_ref[...], shift, scale).astype(_bf16)
    o_ref[...] = x + (0.5 * gate) * _swiglu(h, wgu_ref, wd_ref)
    wa_out[...] = wa_ref[...].astype(_bf16)
    wb_out[...] = wb_ref[...].astype(_bf16)


def _ffn(x2d, mod3, g, wgu, wd, seq, mod_base, cast_a, cast_b):
    n, d = x2d.shape
    tm = TM_FFN
    tpb = seq // tm
    n_steps = n // tm

    def cast_spec(w):
        rows = w.shape[0]
        visits = 1
        while rows % (n_steps // visits) or (rows // (n_steps // visits)) % BF16_SUBLANES:
            visits *= 2
            assert visits <= n_steps
        return pl.BlockSpec((rows // (n_steps // visits), w.shape[1]), lambda i: (i // visits, 0))

    return pl.pallas_call(
        functools.partial(_ffn_kernel, mod_base=mod_base),
        grid=(n_steps,),
        in_specs=[pl.BlockSpec((tm, d), lambda i: (i, 0)),
                  pl.BlockSpec((1, N_MOD, d), lambda i: (i // tpb, 0, 0)),
                  _const_spec((1, d)),
                  _const_spec(wgu.shape),
                  _const_spec(wd.shape),
                  cast_spec(cast_a),
                  cast_spec(cast_b)],
        out_specs=(pl.BlockSpec((tm, d), lambda i: (i, 0)), cast_spec(cast_a), cast_spec(cast_b)),
        out_shape=(jax.ShapeDtypeStruct((n, d), _f32),
                   jax.ShapeDtypeStruct(cast_a.shape, _bf16),
                   jax.ShapeDtypeStruct(cast_b.shape, _bf16)),
        compiler_params=pltpu.CompilerParams(vmem_limit_bytes=VMEM_LIMIT_BYTES),
        name="ffn1",
    )(x2d, mod3, g, wgu, wd, cast_a, cast_b)


def _rope_fm(x, tab):
    q = x.shape[0] // 4
    x1r, x2r, x1c, x2c = x[0:q], x[q:2 * q], x[2 * q:3 * q], x[3 * q:4 * q]
    cr, sr, cc, sc = tab[0:q], tab[q:2 * q], tab[2 * q:3 * q], tab[3 * q:4 * q]
    return jnp.concatenate([x1r * cr - x2r * sr, x1r * sr + x2r * cr,
                            x1c * cc - x2c * sc, x1c * sc + x2c * cc], axis=0)


def _rms_fm(x, g_col, extra=1.0):
    r = lax.rsqrt(jnp.mean(x * x, axis=0, keepdims=True) + EPS)
    if extra != 1.0:
        r = r * extra
    return x * r * g_col


def _mix_pre_kernel(x_ref, mod_ref, g_ref, winT_ref, gq_ref, wuqT_ref, gkv_ref, wukvT_ref,
                    gqh_ref, gkh_ref, t64_ref, t32_ref, qT_ref, k_ref, vT_ref):
    tm = x_ref.shape[0]
    x = x_ref[...]
    h = _norm_mod(x, g_ref[...], mod_ref[0, 3:4, :], mod_ref[0, 4:5, :]).astype(_bf16)
    zT = _dot_nt(winT_ref[...], h)
    q_lat = zT[_OFF[0]:_OFF[1]]
    kv_lat = zT[_OFF[1]:_OFF[2]]
    k_rope = zT[_OFF[2]:_OFF[3]]
    q_g = zT[_OFF[3]:_OFF[4]]
    k_g = zT[_OFF[4]:_OFF[5]]
    v_g = zT[_OFF[5]:_OFF[6]]
    t64 = t64_ref[...]
    t32 = t32_ref[...]

    zeros32 = jnp.zeros((HEAD_PAD - MLA_NOPE_DIM - MLA_ROPE_DIM, tm), _f32)
    zeros64 = jnp.zeros((HEAD_PAD - GQA_HEAD_DIM, tm), _f32)
    ones_pad = (lax.broadcasted_iota(jnp.int32, (V_ROWS - V_DIM, tm), 0) == 0).astype(_f32)

    def put_q(head, qh):
        for t in range(tm // TQ):
            qT_ref[0, head, t] = qh[:, t * TQ:(t + 1) * TQ].astype(_bf16)

    mla_scale = (MLA_NOPE_DIM + MLA_ROPE_DIM) ** -0.5 * LOG2E
    qn = _rms_fm(q_lat, gq_ref[...], mla_scale).astype(_bf16)
    qa = _dot(wuqT_ref[...], qn)
    nope_rows = MLA_HEADS * MLA_NOPE_DIM
    for hd in range(MLA_HEADS):
        nope = qa[hd * MLA_NOPE_DIM:(hd + 1) * MLA_NOPE_DIM]
        pe = _rope_fm(qa[nope_rows + hd * MLA_ROPE_DIM:nope_rows + (hd + 1) * MLA_ROPE_DIM], t32)
        put_q(hd, jnp.concatenate([nope, pe, zeros32], axis=0))

    kvn = _rms_fm(kv_lat, gkv_ref[...]).astype(_bf16)
    kva = _dot(wukvT_ref[...], kvn)
    k_pe = _rope_fm(k_rope, t32)
    for hd in range(MLA_HEADS):
        k_nope = kva[hd * MLA_NOPE_DIM:(hd + 1) * MLA_NOPE_DIM]
        kT = jnp.concatenate([k_nope, k_pe, zeros32], axis=0)
        k_ref[0, hd] = kT.T.astype(_bf16)
        v = kva[nope_rows + hd * MLA_V_DIM:nope_rows + (hd + 1) * MLA_V_DIM]
        vT_ref[0, hd] = jnp.concatenate([v, ones_pad], axis=0).astype(_bf16)

    gqa_scale = GQA_HEAD_DIM ** -0.5 * LOG2E
    for hd in range(GQA_HEADS):
        xh = _rms_fm(q_g[hd * GQA_HEAD_DIM:(hd + 1) * GQA_HEAD_DIM], gqh_ref[...], gqa_scale)
        put_q(MLA_HEADS + hd, jnp.concatenate([_rope_fm(xh, t64), zeros64], axis=0))
    for hd in range(GQA_KV_HEADS):
        xh = _rms_fm(k_g[hd * GQA_HEAD_DIM:(hd + 1) * GQA_HEAD_DIM], gkh_ref[...])
        kT = jnp.concatenate([_rope_fm(xh, t64), zeros64], axis=0)
        k_ref[0, MLA_HEADS + hd] = kT.T.astype(_bf16)
        v = v_g[hd * GQA_HEAD_DIM:(hd + 1) * GQA_HEAD_DIM]
        vT_ref[0, MLA_HEADS + hd] = jnp.concatenate([v, ones_pad], axis=0).astype(_bf16)


def _mix_pre(x2d, mod3, g, winT, gq, wuqT, gkv, wukvT, gqh, gkh, t64, t32, bsz, seq):
    n, d = x2d.shape
    tm = TM_MIX
    tpb = seq // tm
    qpt = tm // TQ
    out_shape = (jax.ShapeDtypeStruct((bsz, N_Q_HEADS, seq // TQ, HEAD_PAD, TQ), _bf16),
                 jax.ShapeDtypeStruct((bsz, N_KV_HEADS, seq, HEAD_PAD), _bf16),
                 jax.ShapeDtypeStruct((bsz, N_KV_HEADS, V_ROWS, seq), _bf16))
    return pl.pallas_call(
        _mix_pre_kernel,
        grid=(n // tm,),
        in_specs=[pl.BlockSpec((tm, d), lambda i: (i, 0)),
                  pl.BlockSpec((1, N_MOD, d), lambda i: (i // tpb, 0, 0)),
                  _const_spec((1, d)),
                  _const_spec(winT.shape),
                  _const_spec(gq.shape),
                  _const_spec(wuqT.shape),
                  _const_spec(gkv.shape),
                  _const_spec(wukvT.shape),
                  _const_spec(gqh.shape),
                  _const_spec(gkh.shape),
                  pl.BlockSpec((GQA_HEAD_DIM, tm), lambda i: (0, i % tpb)),
                  pl.BlockSpec((MLA_ROPE_DIM, tm), lambda i: (0, i % tpb))],
        out_specs=(pl.BlockSpec((1, N_Q_HEADS, qpt, HEAD_PAD, TQ), lambda i: (i // tpb, 0, i % tpb, 0, 0)),
                   pl.BlockSpec((1, N_KV_HEADS, tm, HEAD_PAD), lambda i: (i // tpb, 0, i % tpb, 0)),
                   pl.BlockSpec((1, N_KV_HEADS, V_ROWS, tm), lambda i: (i // tpb, 0, 0, i % tpb))),
        out_shape=out_shape,
        compiler_params=pltpu.CompilerParams(vmem_limit_bytes=VMEM_LIMIT_BYTES),
        name="mix_pre",
    )(x2d, mod3, g, winT, gq, wuqT, gkv, wukvT, gqh, gkh, t64, t32)


def _attn_kernel(q_ref, qn_ref, k_ref, kn_ref, vT_ref, o_ref, s_even, s_odd, m_even, m_odd, acc_ref):
    n_tiles = q_ref.shape[2]
    seq, tq = s_even.shape
    n_chunks = seq // KV_CHUNK

    def score_chunk(q, kk_ref, s_w, c, m_run):
        rows = slice(c * KV_CHUNK, (c + 1) * KV_CHUNK)
        s = _dot(kk_ref[0, 0, rows, :], q)
        s_w[rows, :] = s
        mc = jnp.max(s.reshape(KV_CHUNK // F32_SUBLANES, F32_SUBLANES, tq), axis=0)
        return mc if m_run is None else jnp.maximum(m_run, mc)

    def reduce_chunk(s_r, m, c, acc):
        rows = slice(c * KV_CHUNK, (c + 1) * KV_CHUNK)
        p = jnp.exp2(s_r[rows, :] - m).astype(_bf16)
        part = _dot(vT_ref[0, 0, :, rows], p)
        return part if acc is None else acc + part

    s_bufs, m_bufs = (s_even, s_odd), (m_even, m_odd)

    def col_max(m_run):
        return jnp.max(m_run, axis=0, keepdims=True)

    def finish(acc, t):
        o_ref[0, 0, t] = acc[0:V_DIM] * (1.0 / acc[V_DIM:V_DIM + 1])

    @pl.when(pl.program_id(0) == 0)
    def _():
        m_run = None
        for c in range(n_chunks):
            m_run = score_chunk(q_ref[0, 0, 0], k_ref, s_even, c, m_run)
        m_even[...] = col_max(m_run)

    def run_tiles(j_lo, j_hi):
        m_cur = m_bufs[j_lo % 2][...]
        if j_lo >= 1:
            m_prev, acc_prev = m_bufs[(j_lo - 1) % 2][...], acc_ref[...]
        for j in range(j_lo, j_hi):
            if j + 1 < n_tiles:
                q, kk_ref = q_ref[0, 0, j + 1], k_ref
            else:
                q, kk_ref = qn_ref[0, 0, 0], kn_ref
            s_w, s_r = s_bufs[(j + 1) % 2], s_bufs[j % 2]
            m_run, acc = None, None
            for c in range(n_chunks):
                if c == 0 and j >= 1:
                    finish(reduce_chunk(s_w, m_prev, n_chunks - 1, acc_prev), j - 1)
                m_run = score_chunk(q, kk_ref, s_w, c, m_run)
                if c >= 1:
                    acc = reduce_chunk(s_r, m_cur, c - 1, acc)
            m_prev, m_cur, acc_prev = m_cur, col_max(m_run), acc
        if j_hi == n_tiles:
            finish(reduce_chunk(s_bufs[(j_hi - 1) % 2], m_prev, n_chunks - 1, acc_prev), j_hi - 1)
        else:
            m_bufs[(j_hi - 1) % 2][...] = m_prev
            acc_ref[...] = acc_prev
        m_bufs[j_hi % 2][...] = m_cur

    def first_half(_, carry):
        run_tiles(0, n_tiles // 2)
        return carry

    lax.fori_loop(0, 1 + (pl.program_id(0) >> 30), first_half, 0)
    run_tiles(n_tiles // 2, n_tiles)


def _kv_head(h):
    return jnp.where(h < MLA_HEADS, h, MLA_HEADS + (h - MLA_HEADS) // GQA_GROUP)


def _attention(qT, k, vT):
    bsz, _, n_tiles, _, tq = qT.shape
    seq = n_tiles * tq
    assert n_tiles % 2 == 0
    n_steps = bsz * N_Q_HEADS

    def cur(g):
        return g // N_Q_HEADS, g % N_Q_HEADS

    def nxt(g):
        return cur(jnp.minimum(g + 1, n_steps - 1))

    def q_map(bh):
        return lambda g: (*bh(g), 0, 0, 0)

    def kv_map(bh):
        def index(g):
            b, h = bh(g)
            return b, _kv_head(h), 0, 0
        return index

    return pl.pallas_call(
        _attn_kernel,
        grid=(n_steps,),
        in_specs=[pl.BlockSpec((1, 1, n_tiles, HEAD_PAD, tq), q_map(cur)),
                  pl.BlockSpec((1, 1, 1, HEAD_PAD, tq), q_map(nxt)),
                  pl.BlockSpec((1, 1, seq, HEAD_PAD), kv_map(cur)),
                  pl.BlockSpec((1, 1, seq, HEAD_PAD), kv_map(nxt)),
                  pl.BlockSpec((1, 1, V_ROWS, seq), kv_map(cur))],
        out_specs=pl.BlockSpec((1, 1, n_tiles, V_DIM, tq), q_map(cur)),
        out_shape=jax.ShapeDtypeStruct((bsz, N_Q_HEADS, n_tiles, V_DIM, tq), _f32),
        scratch_shapes=[pltpu.VMEM((seq, tq), _f32), pltpu.VMEM((seq, tq), _f32),
                        pltpu.VMEM((1, tq), _f32), pltpu.VMEM((1, tq), _f32),
                        pltpu.VMEM((V_ROWS, tq), _f32)],
        compiler_params=pltpu.CompilerParams(dimension_semantics=("arbitrary",),
                                             vmem_limit_bytes=VMEM_LIMIT_BYTES),
        name="attn",
    )(qT, qT, k, k, vT)


def _post_kernel(x_ref, o_ref_in, mod_ref, gom_ref, gog_ref, wout_ref, g2_ref, wgu_ref, wd_ref, gf_ref, out_ref,
                 *, final_norm):
    tm = x_ref.shape[0]
    x = x_ref[...]
    oT = o_ref_in[0, :, 0].reshape(N_Q_HEADS * V_DIM, tm)
    half = MLA_HEADS * MLA_V_DIM
    on = jnp.concatenate([_rms_fm(oT[:half], gom_ref[...]), _rms_fm(oT[half:], gog_ref[...])], axis=0)
    y = _dot_tn(on.astype(_bf16), wout_ref[...])
    x = x + mod_ref[0, 5:6, :] * y
    h = _norm_mod(x, g2_ref[...], mod_ref[0, 6:7, :], mod_ref[0, 7:8, :]).astype(_bf16)
    x = x + (0.5 * mod_ref[0, 8:9, :]) * _swiglu(h, wgu_ref, wd_ref)
    if final_norm:
        r = lax.rsqrt(jnp.mean(x * x, axis=-1, keepdims=True) + EPS)
        x = x * r * gf_ref[...]
    out_ref[...] = x


def _post(x2d, oT, mod3, gom, gog, wout, g2, wgu, wd, gf, seq, final_norm):
    n, d = x2d.shape
    tm = oT.shape[-1]
    tpb = seq // tm
    return pl.pallas_call(
        functools.partial(_post_kernel, final_norm=final_norm),
        grid=(n // tm,),
        in_specs=[pl.BlockSpec((tm, d), lambda i: (i, 0)),
                  pl.BlockSpec((1, N_Q_HEADS, 1, V_DIM, tm), lambda i: (i // tpb, 0, i % tpb, 0, 0)),
                  pl.BlockSpec((1, N_MOD, d), lambda i: (i // tpb, 0, 0)),
                  _const_spec(gom.shape),
                  _const_spec(gog.shape),
                  _const_spec(wout.shape),
                  _const_spec((1, d)),
                  _const_spec(wgu.shape),
                  _const_spec(wd.shape),
                  _const_spec((1, d))],
        out_specs=pl.BlockSpec((tm, d), lambda i: (i, 0)),
        out_shape=jax.ShapeDtypeStruct((n, d), _f32),
        compiler_params=pltpu.CompilerParams(vmem_limit_bytes=VMEM_LIMIT_BYTES),
        name="post",
    )(x2d, oT, mod3, gom, gog, wout, g2, wgu, wd, gf)


def _rope_table(seq, dim):
    rows = seq // GRID_W
    row = jnp.repeat(jnp.arange(rows), GRID_W).astype(_f32)
    col = jnp.tile(jnp.arange(GRID_W), rows).astype(_f32)
    axis_dim = dim // 2
    inv_freq = ROPE_THETA ** (-(jnp.arange(axis_dim // 2, dtype=_f32) * 2.0 / axis_dim))
    ang_row = row[:, None] * inv_freq[None, :]
    ang_col = col[:, None] * inv_freq[None, :]
    return jnp.concatenate([jnp.cos(ang_row), jnp.sin(ang_row), jnp.cos(ang_col), jnp.sin(ang_col)], axis=1).T


def kernel(x, c, w_ada, b_ada, g_ffn1, w1_gu, w1_down, g_mix, w_in, g_q_lat, w_uq, g_kv_lat, w_ukv,
           g_qhead, g_khead, g_out_mla, g_out_gqa, w_out, g_ffn2, w2_gu, w2_down, g_final):
    bsz, seq, d = x.shape
    depth = w_ada.shape[0]
    t64 = _rope_table(seq, GQA_HEAD_DIM)
    t32 = _rope_table(seq, MLA_ROPE_DIM)
    x2d = x.reshape(bsz * seq, d)
    for l in range(depth):
        mod3 = _ada(c, w_ada[l], b_ada[l]).reshape(bsz, N_MOD, d)

        x2d, w2_gu_bf, w2_down_bf = _ffn(x2d, mod3, g_ffn1[l].reshape(1, d), w1_gu[l].astype(_bf16),
                                         w1_down[l].astype(_bf16), seq, 0, w2_gu[l], w2_down[l])

        winT = w_in[l].T.astype(_bf16)
        wuq = w_uq[l].reshape(Q_LORA_RANK, MLA_HEADS, MLA_NOPE_DIM + MLA_ROPE_DIM)
        wuqT = jnp.concatenate([wuq[:, :, :MLA_NOPE_DIM].reshape(Q_LORA_RANK, -1),
                                wuq[:, :, MLA_NOPE_DIM:].reshape(Q_LORA_RANK, -1)], axis=1).T.astype(_bf16)
        wukv = w_ukv[l].reshape(KV_LORA_RANK, MLA_HEADS, MLA_NOPE_DIM + MLA_V_DIM)
        wukvT = jnp.concatenate([wukv[:, :, :MLA_NOPE_DIM].reshape(KV_LORA_RANK, -1),
                                 wukv[:, :, MLA_NOPE_DIM:].reshape(KV_LORA_RANK, -1)], axis=1).T.astype(_bf16)
        qT, k, vT = _mix_pre(x2d, mod3, g_mix[l].reshape(1, d), winT,
                             g_q_lat[l].reshape(-1, 1), wuqT, g_kv_lat[l].reshape(-1, 1), wukvT,
                             g_qhead[l].reshape(-1, 1), g_khead[l].reshape(-1, 1), t64, t32, bsz, seq)

        oT = _attention(qT, k, vT)

        x2d = _post(x2d, oT, mod3, g_out_mla[l].reshape(-1, 1), g_out_gqa[l].reshape(-1, 1),
                    w_out[l].astype(_bf16), g_ffn2[l].reshape(1, d),
                    w2_gu_bf, w2_down_bf, g_final.reshape(1, d), seq,
                    final_norm=(l == depth - 1))
    return x2d.reshape(bsz, seq, d)
```

```python
import functools
import math

import jax
import jax.numpy as jnp
import numpy as np
from jax import lax
from jax.experimental import pallas as pl
from jax.experimental.pallas import tpu as pltpu

D_MODEL = 1024
GRID_W = 64
ROPE_THETA = 10000.0
EPS = 1e-6

MLA_HEADS = 8
MLA_NOPE_DIM = 64
MLA_ROPE_DIM = 32
MLA_V_DIM = 64
Q_LORA_RANK = 256
KV_LORA_RANK = 128

GQA_HEADS = 8
GQA_KV_HEADS = 2
GQA_HEAD_DIM = 64
GQA_GROUP = GQA_HEADS // GQA_KV_HEADS

D_FF = 2816
N_MOD = 9

N_Q_HEADS = MLA_HEADS + GQA_HEADS
N_KV_HEADS = MLA_HEADS + GQA_KV_HEADS
HEAD_PAD = 128
V_DIM = 64
F32_SUBLANES = 8
BF16_SUBLANES = 16
V_ROWS = V_DIM + BF16_SUBLANES
LOG2E = math.log2(math.e)

_OFF = np.cumsum([0, Q_LORA_RANK, KV_LORA_RANK, MLA_ROPE_DIM,
                  GQA_HEADS * GQA_HEAD_DIM, GQA_KV_HEADS * GQA_HEAD_DIM, GQA_KV_HEADS * GQA_HEAD_DIM])
D_IN = int(_OFF[-1])

VMEM_LIMIT_BYTES = 60 * 1024 * 1024

ADA_COLS = 1152
TM_FFN = 1024
TQ = 512
TM_MIX = 1024
TM_POST = 1024
MXU_TILE = 256
FF_CHUNK_TILES = (6, 5)
KV_CHUNK = MXU_TILE

_bf16 = jnp.bfloat16
_f32 = jnp.float32


def _dot(a, b):
    return jnp.dot(a, b, preferred_element_type=_f32)


def _dot_nt(a, b):
    return lax.dot_general(a, b, (((1,), (1,)), ((), ())), preferred_element_type=_f32)


def _dot_tn(a, b):
    return lax.dot_general(a, b, (((0,), (0,)), ((), ())), preferred_element_type=_f32)


def _const_spec(shape):
    return pl.BlockSpec(shape, lambda *_: (0,) * len(shape), pipeline_mode=pl.Buffered(1))


def _ada_kernel(c_ref, w_ref, b_ref, o_ref):
    c = c_ref[...]
    ca = c * jax.nn.sigmoid(c)
    w = w_ref[...]
    ca_hi = ca.astype(_bf16)
    ca_lo = (ca - ca_hi.astype(_f32)).astype(_bf16)
    w_hi = w.astype(_bf16)
    w_lo = (w - w_hi.astype(_f32)).astype(_bf16)
    o_ref[...] = _dot(ca_hi, w_hi) + (_dot(ca_hi, w_lo) + _dot(ca_lo, w_hi)) + b_ref[...]


def _ada(c, w_ada, b_ada):
    bsz, d = c.shape
    n = w_ada.shape[1]
    tn = ADA_COLS
    assert n % tn == 0
    return pl.pallas_call(
        _ada_kernel,
        grid=(n // tn,),
        in_specs=[pl.BlockSpec((bsz, d), lambda j: (0, 0)),
                  pl.BlockSpec((d, tn), lambda j: (0, j)),
                  pl.BlockSpec((1, tn), lambda j: (0, j))],
        out_specs=pl.BlockSpec((bsz, tn), lambda j: (0, j)),
        out_shape=jax.ShapeDtypeStruct((bsz, n), _f32),
        compiler_params=pltpu.CompilerParams(vmem_limit_bytes=VMEM_LIMIT_BYTES),
        name="ada",
    )(c, w_ada, b_ada.reshape(1, n))


def _norm_mod(x, g, shift, scale):
    r = lax.rsqrt(jnp.mean(x * x, axis=-1, keepdims=True) + EPS)
    return (x * r) * (g * (1.0 + scale)) + shift


def _swiglu(h, wgu_ref, wd_ref):
    acc = None
    lo = 0
    for tiles in FF_CHUNK_TILES:
        hi = lo + tiles * MXU_TILE
        a = _dot(h, wgu_ref[:, lo:hi])
        b = _dot(h, wgu_ref[:, D_FF + lo:D_FF + hi])
        act = (a * jax.nn.sigmoid(a) * b).astype(_bf16)
        part = _dot(act, wd_ref[lo:hi, :])
        acc = part if acc is None else acc + part
        lo = hi
    assert lo == D_FF
    return acc


def _ffn_kernel(x_ref, mod_ref, g_ref, wgu_ref, wd_ref, wa_ref, wb_ref, o_ref, wa_out, wb_out, *, mod_base):
    shift = mod_ref[0, mod_base:mod_base + 1, :]
    scale = mod_ref[0, mod_base + 1:mod_base + 2, :]
    gate = mod_ref[0, mod_base + 2:mod_base + 3, :]
    x = x_ref[...]
    h = _norm_mod(x, g_ref[...], shift, scale).astype(_bf16)
    o_ref[...] = x + (0.5 * gate) * _swiglu(h, wgu_ref, wd_ref)
    wa_out[...] = wa_ref[...].astype(_bf16)
    wb_out[...] = wb_ref[...].astype(_bf16)


def _ffn(x2d, mod3, g, wgu, wd, seq, mod_base, cast_a, cast_b):
    n, d = x2d.shape
    tm = TM_FFN
    tpb = seq // tm
    n_steps = n // tm

    def cast_spec(w):
        rows = w.shape[0]
        visits = 1
        while rows % (n_steps // visits) or (rows // (n_steps // visits)) % BF16_SUBLANES:
            visits *= 2
            assert visits <= n_steps
        return pl.BlockSpec((rows // (n_steps // visits), w.shape[1]), lambda i: (i // visits, 0))

    return pl.pallas_call(
        functools.partial(_ffn_kernel, mod_base=mod_base),
        grid=(n_steps,),
        in_specs=[pl.BlockSpec((tm, d), lambda i: (i, 0)),
                  pl.BlockSpec((1, N_MOD, d), lambda i: (i // tpb, 0, 0)),
                  _const_spec((1, d)),
                  _const_spec(wgu.shape),
                  _const_spec(wd.shape),
                  cast_spec(cast_a),
                  cast_spec(cast_b)],
        out_specs=(pl.BlockSpec((tm, d), lambda i: (i, 0)), cast_spec(cast_a), cast_spec(cast_b)),
        out_shape=(jax.ShapeDtypeStruct((n, d), _f32),
                   jax.ShapeDtypeStruct(cast_a.shape, _bf16),
                   jax.ShapeDtypeStruct(cast_b.shape, _bf16)),
        compiler_params=pltpu.CompilerParams(vmem_limit_bytes=VMEM_LIMIT_BYTES),
        name="ffn1",
    )(x2d, mod3, g, wgu, wd, cast_a, cast_b)


def _rope_fm(x, tab):
    q = x.shape[0] // 4
    x1r, x2r, x1c, x2c = x[0:q], x[q:2 * q], x[2 * q:3 * q], x[3 * q:4 * q]
    cr, sr, cc, sc = tab[0:q], tab[q:2 * q], tab[2 * q:3 * q], tab[3 * q:4 * q]
    return jnp.concatenate([x1r * cr - x2r * sr, x1r * sr + x2r * cr,
                            x1c * cc - x2c * sc, x1c * sc + x2c * cc], axis=0)


def _rms_fm(x, g_col, extra=1.0):
    r = lax.rsqrt(jnp.mean(x * x, axis=0, keepdims=True) + EPS)
    if extra != 1.0:
        r = r * extra
    return x * r * g_col


def _mix_pre_kernel(x_ref, mod_ref, g_ref, winT_ref, gq_ref, wuqT_ref, gkv_ref, wukvT_ref,
                    gqh_ref, gkh_ref, t64_ref, t32_ref, qT_ref, k_ref, vT_ref):
    tm = x_ref.shape[0]
    x = x_ref[...]
    h = _norm_mod(x, g_ref[...], mod_ref[0, 3:4, :], mod_ref[0, 4:5, :]).astype(_bf16)
    zT = _dot_nt(winT_ref[...], h)
    q_lat = zT[_OFF[0]:_OFF[1]]
    kv_lat = zT[_OFF[1]:_OFF[2]]
    k_rope = zT[_OFF[2]:_OFF[3]]
    q_g = zT[_OFF[3]:_OFF[4]]
    k_g = zT[_OFF[4]:_OFF[5]]
    v_g = zT[_OFF[5]:_OFF[6]]
    t64 = t64_ref[...]
    t32 = t32_ref[...]

    zeros32 = jnp.zeros((HEAD_PAD - MLA_NOPE_DIM - MLA_ROPE_DIM, tm), _f32)
    zeros64 = jnp.zeros((HEAD_PAD - GQA_HEAD_DIM, tm), _f32)
    ones_pad = (lax.broadcasted_iota(jnp.int32, (V_ROWS - V_DIM, tm), 0) == 0).astype(_f32)

    def put_q(head, qh):
        for t in range(tm // TQ):
            qT_ref[0, head, t] = qh[:, t * TQ:(t + 1) * TQ].astype(_bf16)

    mla_scale = (MLA_NOPE_DIM + MLA_ROPE_DIM) ** -0.5 * LOG2E
    qn = _rms_fm(q_lat, gq_ref[...], mla_scale).astype(_bf16)
    qa = _dot(wuqT_ref[...], qn)
    nope_rows = MLA_HEADS * MLA_NOPE_DIM
    for hd in range(MLA_HEADS):
        nope = qa[hd * MLA_NOPE_DIM:(hd + 1) * MLA_NOPE_DIM]
        pe = _rope_fm(qa[nope_rows + hd * MLA_ROPE_DIM:nope_rows + (hd + 1) * MLA_ROPE_DIM], t32)
        put_q(hd, jnp.concatenate([nope, pe, zeros32], axis=0))

    kvn = _rms_fm(kv_lat, gkv_ref[...]).astype(_bf16)
    kva = _dot(wukvT_ref[...], kvn)
    k_pe = _rope_fm(k_rope, t32)
    for hd in range(MLA_HEADS):
        k_nope = kva[hd * MLA_NOPE_DIM:(hd + 1) * MLA_NOPE_DIM]
        kT = jnp.concatenate([k_nope, k_pe, zeros32], axis=0)
        k_ref[0, hd] = kT.T.astype(_bf16)
        v = kva[nope_rows + hd * MLA_V_DIM:nope_rows + (hd + 1) * MLA_V_DIM]
        vT_ref[0, hd] = jnp.concatenate([v, ones_pad], axis=0).astype(_bf16)

    gqa_scale = GQA_HEAD_DIM ** -0.5 * LOG2E
    for hd in range(GQA_HEADS):
        xh = _rms_fm(q_g[hd * GQA_HEAD_DIM:(hd + 1) * GQA_HEAD_DIM], gqh_ref[...], gqa_scale)
        put_q(MLA_HEADS + hd, jnp.concatenate([_rope_fm(xh, t64), zeros64], axis=0))
    for hd in range(GQA_KV_HEADS):
        xh = _rms_fm(k_g[hd * GQA_HEAD_DIM:(hd + 1) * GQA_HEAD_DIM], gkh_ref[...])
        kT = jnp.concatenate([_rope_fm(xh, t64), zeros64], axis=0)
        k_ref[0, MLA_HEADS + hd] = kT.T.astype(_bf16)
        v = v_g[hd * GQA_HEAD_DIM:(hd + 1) * GQA_HEAD_DIM]
        vT_ref[0, MLA_HEADS + hd] = jnp.concatenate([v, ones_pad], axis=0).astype(_bf16)


def _mix_pre(x2d, mod3, g, winT, gq, wuqT, gkv, wukvT, gqh, gkh, t64, t32, bsz, seq):
    n, d = x2d.shape
    tm = TM_MIX
    tpb = seq // tm
    qpt = tm // TQ
    out_shape = (jax.ShapeDtypeStruct((bsz, N_Q_HEADS, seq // TQ, HEAD_PAD, TQ), _bf16),
                 jax.ShapeDtypeStruct((bsz, N_KV_HEADS, seq, HEAD_PAD), _bf16),
                 jax.ShapeDtypeStruct((bsz, N_KV_HEADS, V_ROWS, seq), _bf16))
    return pl.pallas_call(
        _mix_pre_kernel,
        grid=(n // tm,),
        in_specs=[pl.BlockSpec((tm, d), lambda i: (i, 0)),
                  pl.BlockSpec((1, N_MOD, d), lambda i: (i // tpb, 0, 0)),
                  _const_spec((1, d)),
                  _const_spec(winT.shape),
                  _const_spec(gq.shape),
                  _const_spec(wuqT.shape),
                  _const_spec(gkv.shape),
                  _const_spec(wukvT.shape),
                  _const_spec(gqh.shape),
                  _const_spec(gkh.shape),
                  pl.BlockSpec((GQA_HEAD_DIM, tm), lambda i: (0, i % tpb)),
                  pl.BlockSpec((MLA_ROPE_DIM, tm), lambda i: (0, i % tpb))],
        out_specs=(pl.BlockSpec((1, N_Q_HEADS, qpt, HEAD_PAD, TQ), lambda i: (i // tpb, 0, i % tpb, 0, 0)),
                   pl.BlockSpec((1, N_KV_HEADS, tm, HEAD_PAD), lambda i: (i // tpb, 0, i % tpb, 0)),
                   pl.BlockSpec((1, N_KV_HEADS, V_ROWS, tm), lambda i: (i // tpb, 0, 0, i % tpb))),
        out_shape=out_shape,
        compiler_params=pltpu.CompilerParams(vmem_limit_bytes=VMEM_LIMIT_BYTES),
        name="mix_pre",
    )(x2d, mod3, g, winT, gq, wuqT, gkv, wukvT, gqh, gkh, t64, t32)


def _attn_kernel(q_ref, qn_ref, k_ref, kn_ref, vT_ref, o_ref, s_even, s_odd, m_even, m_odd, acc_ref):
    n_tiles = q_ref.shape[2]
    seq, tq = s_even.shape
    n_chunks = seq // KV_CHUNK

    def score_chunk(q, kk_ref, s_w, c, m_run):
        rows = slice(c * KV_CHUNK, (c + 1) * KV_CHUNK)
        s = _dot(kk_ref[0, 0, rows, :], q)
        s_w[rows, :] = s
        mc = jnp.max(s.reshape(KV_CHUNK // F32_SUBLANES, F32_SUBLANES, tq), axis=0)
        return mc if m_run is None else jnp.maximum(m_run, mc)

    def reduce_chunk(s_r, m, c, acc):
        rows = slice(c * KV_CHUNK, (c + 1) * KV_CHUNK)
        p = jnp.exp2(s_r[rows, :] - m).astype(_bf16)
        part = _dot(vT_ref[0, 0, :, rows], p)
        return part if acc is None else acc + part

    s_bufs, m_bufs = (s_even, s_odd), (m_even, m_odd)

    def col_max(m_run):
        return jnp.max(m_run, axis=0, keepdims=True)

    def finish(acc, t):
        o_ref[0, 0, t] = acc[0:V_DIM] * (1.0 / acc[V_DIM:V_DIM + 1])

    @pl.when(pl.program_id(0) == 0)
    def _():
        m_run = None
        for c in range(n_chunks):
            m_run = score_chunk(q_ref[0, 0, 0], k_ref, s_even, c, m_run)
        m_even[...] = col_max(m_run)

    def run_tiles(j_lo, j_hi):
        m_cur = m_bufs[j_lo % 2][...]
        if j_lo >= 1:
            m_prev, acc_prev = m_bufs[(j_lo - 1) % 2][...], acc_ref[...]
        for j in range(j_lo, j_hi):
            if j + 1 < n_tiles:
                q, kk_ref = q_ref[0, 0, j + 1], k_ref
            else:
                q, kk_ref = qn_ref[0, 0, 0], kn_ref
            s_w, s_r = s_bufs[(j + 1) % 2], s_bufs[j % 2]
            m_run, acc = None, None
            for c in range(n_chunks):
                if c == 0 and j >= 1:
                    finish(reduce_chunk(s_w, m_prev, n_chunks - 1, acc_prev), j - 1)
                m_run = score_chunk(q, kk_ref, s_w, c, m_run)
                if c >= 1:
                    acc = reduce_chunk(s_r, m_cur, c - 1, acc)
            m_prev, m_cur, acc_prev = m_cur, col_max(m_run), acc
        if j_hi == n_tiles:
            finish(reduce_chunk(s_bufs[(j_hi - 1) % 2], m_prev, n_chunks - 1, acc_prev), j_hi - 1)
        else:
            m_bufs[(j_hi - 1) % 2][...] = m_prev
            acc_ref[...] = acc_prev
        m_bufs[j_hi % 2][...] = m_cur

    def first_half(_, carry):
        run_tiles(0, n_tiles // 2)
        return carry

    lax.fori_loop(0, 1 + (pl.program_id(0) >> 30), first_half, 0)
    run_tiles(n_tiles // 2, n_tiles)


def _kv_head(h):
    return jnp.where(h < MLA_HEADS, h, MLA_HEADS + (h - MLA_HEADS) // GQA_GROUP)


def _attention(qT, k, vT):
    bsz, _, n_tiles, _, tq = qT.shape
    seq = n_tiles * tq
    assert n_tiles % 2 == 0
    n_steps = bsz * N_Q_HEADS

    def cur(g):
        return g // N_Q_HEADS, g % N_Q_HEADS

    def nxt(g):
        return cur(jnp.minimum(g + 1, n_steps - 1))

    def q_map(bh):
        return lambda g: (*bh(g), 0, 0, 0)

    def kv_map(bh):
        def index(g):
            b, h = bh(g)
            return b, _kv_head(h), 0, 0
        return index

    return pl.pallas_call(
        _attn_kernel,
        grid=(n_steps,),
        in_specs=[pl.BlockSpec((1, 1, n_tiles, HEAD_PAD, tq), q_map(cur)),
                  pl.BlockSpec((1, 1, 1, HEAD_PAD, tq), q_map(nxt)),
                  pl.BlockSpec((1, 1, seq, HEAD_PAD), kv_map(cur)),
                  pl.BlockSpec((1, 1, seq, HEAD_PAD), kv_map(nxt)),
                  pl.BlockSpec((1, 1, V_ROWS, seq), kv_map(cur))],
        out_specs=pl.BlockSpec((1, 1, n_tiles, V_DIM, tq), q_map(cur)),
        out_shape=jax.ShapeDtypeStruct((bsz, N_Q_HEADS, n_tiles, V_DIM, tq), _f32),
        scratch_shapes=[pltpu.VMEM((seq, tq), _f32), pltpu.VMEM((seq, tq), _f32),
                        pltpu.VMEM((1, tq), _f32), pltpu.VMEM((1, tq), _f32),
                        pltpu.VMEM((V_ROWS, tq), _f32)],
        compiler_params=pltpu.CompilerParams(dimension_semantics=("arbitrary",),
                                             vmem_limit_bytes=VMEM_LIMIT_BYTES),
        name="attn",
    )(qT, qT, k, k, vT)


def _post_kernel(x_ref, o_ref_in, mod_ref, gom_ref, gog_ref, wout_ref, g2_ref, wgu_ref, wd_ref, gf_ref, out_ref,
                 *, final_norm):
    tm = x_ref.shape[0]
    x = x_ref[...]
    oT = jnp.concatenate([o_ref_in[0, :, t].reshape(N_Q_HEADS * V_DIM, TQ) for t in range(tm // TQ)],
                         axis=1)
    half = MLA_HEADS * MLA_V_DIM
    on = jnp.concatenate([_rms_fm(oT[:half], gom_ref[...]), _rms_fm(oT[half:], gog_ref[...])], axis=0)
    y = _dot_tn(on.astype(_bf16), wout_ref[...])
    x = x + mod_ref[0, 5:6, :] * y
    h = _norm_mod(x, g2_ref[...], mod_ref[0, 6:7, :], mod_ref[0, 7:8, :]).astype(_bf16)
    x = x + (0.5 * mod_ref[0, 8:9, :]) * _swiglu(h, wgu_ref, wd_ref)
    if final_norm:
        r = lax.rsqrt(jnp.mean(x * x, axis=-1, keepdims=True) + EPS)
        x = x * r * gf_ref[...]
    out_ref[...] = x


def _post(x2d, oT, mod3, gom, gog, wout, g2, wgu, wd, gf, seq, final_norm):
    n, d = x2d.shape
    tm = TM_POST
    tpb = seq // tm
    return pl.pallas_call(
        functools.partial(_post_kernel, final_norm=final_norm),
        grid=(n // tm,),
        in_specs=[pl.BlockSpec((tm, d), lambda i: (i, 0)),
                  pl.BlockSpec((1, N_Q_HEADS, tm // TQ, V_DIM, TQ), lambda i: (i // tpb, 0, i % tpb, 0, 0)),
                  pl.BlockSpec((1, N_MOD, d), lambda i: (i // tpb, 0, 0)),
                  _const_spec(gom.shape),
                  _const_spec(gog.shape),
                  _const_spec(wout.shape),
                  _const_spec((1, d)),
                  _const_spec(wgu.shape),
                  _const_spec(wd.shape),
                  _const_spec((1, d))],
        out_specs=pl.BlockSpec((tm, d), lambda i: (i, 0)),
        out_shape=jax.ShapeDtypeStruct((n, d), _f32),
        compiler_params=pltpu.CompilerParams(vmem_limit_bytes=VMEM_LIMIT_BYTES),
        name="post",
    )(x2d, oT, mod3, gom, gog, wout, g2, wgu, wd, gf)


def _rope_table(seq, dim):
    rows = seq // GRID_W
    row = jnp.repeat(jnp.arange(rows), GRID_W).astype(_f32)
    col = jnp.tile(jnp.arange(GRID_W), rows).astype(_f32)
    axis_dim = dim // 2
    inv_freq = ROPE_THETA ** (-(jnp.arange(axis_dim // 2, dtype=_f32) * 2.0 / axis_dim))
    ang_row = row[:, None] * inv_freq[None, :]
    ang_col = col[:, None] * inv_freq[None, :]
    return jnp.concatenate([jnp.cos(ang_row), jnp.sin(ang_row), jnp.cos(ang_col), jnp.sin(ang_col)], axis=1).T


def kernel(x, c, w_ada, b_ada, g_ffn1, w1_gu, w1_down, g_mix, w_in, g_q_lat, w_uq, g_kv_lat, w_ukv,
           g_qhead, g_khead, g_out_mla, g_out_gqa, w_out, g_ffn2, w2_gu, w2_down, g_final):
    bsz, seq, d = x.shape
    depth = w_ada.shape[0]
    t64 = _rope_table(seq, GQA_HEAD_DIM)
    t32 = _rope_table(seq, MLA_ROPE_DIM)
    x2d = x.reshape(bsz * seq, d)
    for l in range(depth):
        mod3 = _ada(c, w_ada[l], b_ada[l]).reshape(bsz, N_MOD, d)

        x2d, w2_gu_bf, w2_down_bf = _ffn(x2d, mod3, g_ffn1[l].reshape(1, d), w1_gu[l].astype(_bf16),
                                         w1_down[l].astype(_bf16), seq, 0, w2_gu[l], w2_down[l])

        winT = w_in[l].T.astype(_bf16)
        wuq = w_uq[l].reshape(Q_LORA_RANK, MLA_HEADS, MLA_NOPE_DIM + MLA_ROPE_DIM)
        wuqT = jnp.concatenate([wuq[:, :, :MLA_NOPE_DIM].reshape(Q_LORA_RANK, -1),
                                wuq[:, :, MLA_NOPE_DIM:].reshape(Q_LORA_RANK, -1)], axis=1).T.astype(_bf16)
        wukv = w_ukv[l].reshape(KV_LORA_RANK, MLA_HEADS, MLA_NOPE_DIM + MLA_V_DIM)
        wukvT = jnp.concatenate([wukv[:, :, :MLA_NOPE_DIM].reshape(KV_LORA_RANK, -1),
                                 wukv[:, :, MLA_NOPE_DIM:].reshape(KV_LORA_RANK, -1)], axis=1).T.astype(_bf16)
        qT, k, vT = _mix_pre(x2d, mod3, g_mix[l].reshape(1, d), winT,
                             g_q_lat[l].reshape(-1, 1), wuqT, g_kv_lat[l].reshape(-1, 1), wukvT,
                             g_qhead[l].reshape(-1, 1), g_khead[l].reshape(-1, 1), t64, t32, bsz, seq)

        oT = _attention(qT, k, vT)

        x2d = _post(x2d, oT, mod3, g_out_mla[l].reshape(-1, 1), g_out_gqa[l].reshape(-1, 1),
                    w_out[l].astype(_bf16), g_ffn2[l].reshape(1, d),
                    w2_gu_bf, w2_down_bf, g_final.reshape(1, d), seq,
                    final_norm=(l == depth - 1))
    return x2d.reshape(bsz, seq, d)
```

```python
import functools
import math

import jax
import jax.numpy as jnp
import numpy as np
from jax import lax
from jax.experimental import pallas as pl
from jax.experimental.pallas import tpu as pltpu

D_MODEL = 1024
GRID_W = 64
ROPE_THETA = 10000.0
EPS = 1e-6

MLA_HEADS = 8
MLA_NOPE_DIM = 64
MLA_ROPE_DIM = 32
MLA_V_DIM = 64
Q_LORA_RANK = 256
KV_LORA_RANK = 128

GQA_HEADS = 8
GQA_KV_HEADS = 2
GQA_HEAD_DIM = 64
GQA_GROUP = GQA_HEADS // GQA_KV_HEADS

D_FF = 2816
N_MOD = 9

N_Q_HEADS = MLA_HEADS + GQA_HEADS
N_KV_HEADS = MLA_HEADS + GQA_KV_HEADS
HEAD_PAD = 128
V_DIM = 64
F32_SUBLANES = 8
BF16_SUBLANES = 16
V_ROWS = V_DIM + BF16_SUBLANES
LOG2E = math.log2(math.e)

_OFF = np.cumsum([0, Q_LORA_RANK, KV_LORA_RANK, MLA_ROPE_DIM,
                  GQA_HEADS * GQA_HEAD_DIM, GQA_KV_HEADS * GQA_HEAD_DIM, GQA_KV_HEADS * GQA_HEAD_DIM])
D_IN = int(_OFF[-1])

VMEM_LIMIT_BYTES = 60 * 1024 * 1024

ADA_COLS = 1152
TM_FFN = 1024
W_STAGE_COLS = 512
W_STAGE_ROWS = 256
TQ = 512
TM_MIX = 1024
TM_POST = 1024
MXU_TILE = 256
FF_CHUNK_TILES = (6, 5)
KV_CHUNK = MXU_TILE

_bf16 = jnp.bfloat16
_f32 = jnp.float32


def _dot(a, b):
    return jnp.dot(a, b, preferred_element_type=_f32)


def _dot_nt(a, b):
    return lax.dot_general(a, b, (((1,), (1,)), ((), ())), preferred_element_type=_f32)


def _dot_tn(a, b):
    return lax.dot_general(a, b, (((0,), (0,)), ((), ())), preferred_element_type=_f32)


def _const_spec(shape):
    return pl.BlockSpec(shape, lambda *_: (0,) * len(shape), pipeline_mode=pl.Buffered(1))


def _ada_kernel(c_ref, w_ref, b_ref, o_ref):
    c = c_ref[...]
    ca = c * jax.nn.sigmoid(c)
    w = w_ref[...]
    ca_hi = ca.astype(_bf16)
    ca_lo = (ca - ca_hi.astype(_f32)).astype(_bf16)
    w_hi = w.astype(_bf16)
    w_lo = (w - w_hi.astype(_f32)).astype(_bf16)
    o_ref[...] = _dot(ca_hi, w_hi) + (_dot(ca_hi, w_lo) + _dot(ca_lo, w_hi)) + b_ref[...]


def _ada(c, w_ada, b_ada):
    bsz, d = c.shape
    n = w_ada.shape[1]
    tn = ADA_COLS
    assert n % tn == 0
    return pl.pallas_call(
        _ada_kernel,
        grid=(n // tn,),
        in_specs=[pl.BlockSpec((bsz, d), lambda j: (0, 0)),
                  pl.BlockSpec((d, tn), lambda j: (0, j)),
                  pl.BlockSpec((1, tn), lambda j: (0, j))],
        out_specs=pl.BlockSpec((bsz, tn), lambda j: (0, j)),
        out_shape=jax.ShapeDtypeStruct((bsz, n), _f32),
        compiler_params=pltpu.CompilerParams(vmem_limit_bytes=VMEM_LIMIT_BYTES),
        name="ada",
    )(c, w_ada, b_ada.reshape(1, n))


def _norm_mod(x, g, shift, scale):
    r = lax.rsqrt(jnp.mean(x * x, axis=-1, keepdims=True) + EPS)
    return (x * r) * (g * (1.0 + scale)) + shift


def _swiglu(h, wgu_ref, wd_ref):
    acc = None
    lo = 0
    for tiles in FF_CHUNK_TILES:
        hi = lo + tiles * MXU_TILE
        a = _dot(h, wgu_ref[:, lo:hi])
        b = _dot(h, wgu_ref[:, D_FF + lo:D_FF + hi])
        act = (a * jax.nn.sigmoid(a) * b).astype(_bf16)
        part = _dot(act, wd_ref[lo:hi, :])
        acc = part if acc is None else acc + part
        lo = hi
    assert lo == D_FF
    return acc


def _stage_cast(n_chunks, copy, consume):
    copy(0, 0).start()
    for c in range(n_chunks):
        slot = c % 2
        if c + 1 < n_chunks:
            copy(c + 1, 1 - slot).start()
        copy(c, slot).wait()
        consume(c, slot)


def _ffn_kernel(x_ref, mod_ref, g_ref, wgu_hbm, wd_hbm, wa_ref, wb_ref, o_ref, wa_out, wb_out,
                wgu_ref, wd_ref, stage_gu, stage_d, sem_gu, sem_d, *, mod_base):
    @pl.when(pl.program_id(0) == 0)
    def _():
        gu_cols, d_rows = stage_gu.shape[2], stage_d.shape[1]

        def gu_copy(c, slot):
            return pltpu.make_async_copy(wgu_hbm.at[:, pl.ds(c * gu_cols, gu_cols)], stage_gu.at[slot],
                                         sem_gu.at[slot])

        def gu_consume(c, slot):
            wgu_ref[:, c * gu_cols:(c + 1) * gu_cols] = stage_gu[slot].astype(_bf16)

        def d_copy(c, slot):
            return pltpu.make_async_copy(wd_hbm.at[pl.ds(c * d_rows, d_rows), :], stage_d.at[slot],
                                         sem_d.at[slot])

        def d_consume(c, slot):
            wd_ref[c * d_rows:(c + 1) * d_rows, :] = stage_d[slot].astype(_bf16)

        _stage_cast(wgu_ref.shape[1] // gu_cols, gu_copy, gu_consume)
        _stage_cast(wd_ref.shape[0] // d_rows, d_copy, d_consume)

    shift = mod_ref[0, mod_base:mod_base + 1, :]
    scale = mod_ref[0, mod_base + 1:mod_base + 2, :]
    gate = mod_ref[0, mod_base + 2:mod_base + 3, :]
    x = x_ref[...]
    h = _norm_mod(x, g_ref[...], shift, scale).astype(_bf16)
    o_ref[...] = x + (0.5 * gate) * _swiglu(h, wgu_ref, wd_ref)
    wa_out[...] = wa_ref[...].astype(_bf16)
    wb_out[...] = wb_ref[...].astype(_bf16)


def _ffn(x2d, mod3, g, wgu, wd, seq, mod_base, cast_a, cast_b):
    n, d = x2d.shape
    tm = TM_FFN
    tpb = seq // tm
    n_steps = n // tm
    assert wgu.shape[1] % W_STAGE_COLS == 0 and wd.shape[0] % W_STAGE_ROWS == 0

    def cast_spec(w):
        rows = w.shape[0]
        visits = 1
        while rows % (n_steps // visits) or (rows // (n_steps // visits)) % BF16_SUBLANES:
            visits *= 2
            assert visits <= n_steps
        return pl.BlockSpec((rows // (n_steps // visits), w.shape[1]), lambda i: (i // visits, 0))

    return pl.pallas_call(
        functools.partial(_ffn_kernel, mod_base=mod_base),
        grid=(n_steps,),
        in_specs=[pl.BlockSpec((tm, d), lambda i: (i, 0)),
                  pl.BlockSpec((1, N_MOD, d), lambda i: (i // tpb, 0, 0)),
                  _const_spec((1, d)),
                  pl.BlockSpec(memory_space=pl.ANY),
                  pl.BlockSpec(memory_space=pl.ANY),
                  cast_spec(cast_a),
                  cast_spec(cast_b)],
        out_specs=(pl.BlockSpec((tm, d), lambda i: (i, 0)), cast_spec(cast_a), cast_spec(cast_b)),
        out_shape=(jax.ShapeDtypeStruct((n, d), _f32),
                   jax.ShapeDtypeStruct(cast_a.shape, _bf16),
                   jax.ShapeDtypeStruct(cast_b.shape, _bf16)),
        scratch_shapes=[pltpu.VMEM(wgu.shape, _bf16), pltpu.VMEM(wd.shape, _bf16),
                        pltpu.VMEM((2, wgu.shape[0], W_STAGE_COLS), _f32),
                        pltpu.VMEM((2, W_STAGE_ROWS, wd.shape[1]), _f32),
                        pltpu.SemaphoreType.DMA((2,)), pltpu.SemaphoreType.DMA((2,))],
        compiler_params=pltpu.CompilerParams(dimension_semantics=("arbitrary",),
                                             vmem_limit_bytes=VMEM_LIMIT_BYTES),
        name="ffn1",
    )(x2d, mod3, g, wgu, wd, cast_a, cast_b)


def _rope_fm(x, tab):
    q = x.shape[0] // 4
    x1r, x2r, x1c, x2c = x[0:q], x[q:2 * q], x[2 * q:3 * q], x[3 * q:4 * q]
    cr, sr, cc, sc = tab[0:q], tab[q:2 * q], tab[2 * q:3 * q], tab[3 * q:4 * q]
    return jnp.concatenate([x1r * cr - x2r * sr, x1r * sr + x2r * cr,
                            x1c * cc - x2c * sc, x1c * sc + x2c * cc], axis=0)


def _rms_fm(x, g_col, extra=1.0):
    r = lax.rsqrt(jnp.mean(x * x, axis=0, keepdims=True) + EPS)
    if extra != 1.0:
        r = r * extra
    return x * r * g_col


def _mix_pre_kernel(x_ref, mod_ref, g_ref, winT_ref, gq_ref, wuqT_ref, gkv_ref, wukvT_ref,
                    gqh_ref, gkh_ref, t64_ref, t32_ref, qT_ref, k_ref, vT_ref):
    tm = x_ref.shape[0]
    x = x_ref[...]
    h = _norm_mod(x, g_ref[...], mod_ref[0, 3:4, :], mod_ref[0, 4:5, :]).astype(_bf16)
    zT = _dot_nt(winT_ref[...], h)
    q_lat = zT[_OFF[0]:_OFF[1]]
    kv_lat = zT[_OFF[1]:_OFF[2]]
    k_rope = zT[_OFF[2]:_OFF[3]]
    q_g = zT[_OFF[3]:_OFF[4]]
    k_g = zT[_OFF[4]:_OFF[5]]
    v_g = zT[_OFF[5]:_OFF[6]]
    t64 = t64_ref[...]
    t32 = t32_ref[...]

    zeros32 = jnp.zeros((HEAD_PAD - MLA_NOPE_DIM - MLA_ROPE_DIM, tm), _f32)
    zeros64 = jnp.zeros((HEAD_PAD - GQA_HEAD_DIM, tm), _f32)
    ones_pad = (lax.broadcasted_iota(jnp.int32, (V_ROWS - V_DIM, tm), 0) == 0).astype(_f32)

    def put_q(head, qh):
        for t in range(tm // TQ):
            qT_ref[0, head, t] = qh[:, t * TQ:(t + 1) * TQ].astype(_bf16)

    mla_scale = (MLA_NOPE_DIM + MLA_ROPE_DIM) ** -0.5 * LOG2E
    qn = _rms_fm(q_lat, gq_ref[...], mla_scale).astype(_bf16)
    qa = _dot(wuqT_ref[...], qn)
    nope_rows = MLA_HEADS * MLA_NOPE_DIM
    for hd in range(MLA_HEADS):
        nope = qa[hd * MLA_NOPE_DIM:(hd + 1) * MLA_NOPE_DIM]
        pe = _rope_fm(qa[nope_rows + hd * MLA_ROPE_DIM:nope_rows + (hd + 1) * MLA_ROPE_DIM], t32)
        put_q(hd, jnp.concatenate([nope, pe, zeros32], axis=0))

    kvn = _rms_fm(kv_lat, gkv_ref[...]).astype(_bf16)
    kva = _dot(wukvT_ref[...], kvn)
    k_pe = _rope_fm(k_rope, t32)
    for hd in range(MLA_HEADS):
        k_nope = kva[hd * MLA_NOPE_DIM:(hd + 1) * MLA_NOPE_DIM]
        kT = jnp.concatenate([k_nope, k_pe, zeros32], axis=0)
        k_ref[0, hd] = kT.T.astype(_bf16)
        v = kva[nope_rows + hd * MLA_V_DIM:nope_rows + (hd + 1) * MLA_V_DIM]
        vT_ref[0, hd] = jnp.concatenate([v, ones_pad], axis=0).astype(_bf16)

    gqa_scale = GQA_HEAD_DIM ** -0.5 * LOG2E
    for hd in range(GQA_HEADS):
        xh = _rms_fm(q_g[hd * GQA_HEAD_DIM:(hd + 1) * GQA_HEAD_DIM], gqh_ref[...], gqa_scale)
        put_q(MLA_HEADS + hd, jnp.concatenate([_rope_fm(xh, t64), zeros64], axis=0))
    for hd in range(GQA_KV_HEADS):
        xh = _rms_fm(k_g[hd * GQA_HEAD_DIM:(hd + 1) * GQA_HEAD_DIM], gkh_ref[...])
        kT = jnp.concatenate([_rope_fm(xh, t64), zeros64], axis=0)
        k_ref[0, MLA_HEADS + hd] = kT.T.astype(_bf16)
        v = v_g[hd * GQA_HEAD_DIM:(hd + 1) * GQA_HEAD_DIM]
        vT_ref[0, MLA_HEADS + hd] = jnp.concatenate([v, ones_pad], axis=0).astype(_bf16)


def _mix_pre(x2d, mod3, g, winT, gq, wuqT, gkv, wukvT, gqh, gkh, t64, t32, bsz, seq):
    n, d = x2d.shape
    tm = TM_MIX
    tpb = seq // tm
    qpt = tm // TQ
    out_shape = (jax.ShapeDtypeStruct((bsz, N_Q_HEADS, seq // TQ, HEAD_PAD, TQ), _bf16),
                 jax.ShapeDtypeStruct((bsz, N_KV_HEADS, seq, HEAD_PAD), _bf16),
                 jax.ShapeDtypeStruct((bsz, N_KV_HEADS, V_ROWS, seq), _bf16))
    return pl.pallas_call(
        _mix_pre_kernel,
        grid=(n // tm,),
        in_specs=[pl.BlockSpec((tm, d), lambda i: (i, 0)),
                  pl.BlockSpec((1, N_MOD, d), lambda i: (i // tpb, 0, 0)),
                  _const_spec((1, d)),
                  _const_spec(winT.shape),
                  _const_spec(gq.shape),
                  _const_spec(wuqT.shape),
                  _const_spec(gkv.shape),
                  _const_spec(wukvT.shape),
                  _const_spec(gqh.shape),
                  _const_spec(gkh.shape),
                  pl.BlockSpec((GQA_HEAD_DIM, tm), lambda i: (0, i % tpb)),
                  pl.BlockSpec((MLA_ROPE_DIM, tm), lambda i: (0, i % tpb))],
        out_specs=(pl.BlockSpec((1, N_Q_HEADS, qpt, HEAD_PAD, TQ), lambda i: (i // tpb, 0, i % tpb, 0, 0)),
                   pl.BlockSpec((1, N_KV_HEADS, tm, HEAD_PAD), lambda i: (i // tpb, 0, i % tpb, 0)),
                   pl.BlockSpec((1, N_KV_HEADS, V_ROWS, tm), lambda i: (i // tpb, 0, 0, i % tpb))),
        out_shape=out_shape,
        compiler_params=pltpu.CompilerParams(vmem_limit_bytes=VMEM_LIMIT_BYTES),
        name="mix_pre",
    )(x2d, mod3, g, winT, gq, wuqT, gkv, wukvT, gqh, gkh, t64, t32)


def _attn_kernel(q_ref, qn_ref, k_ref, kn_ref, vT_ref, o_ref, s_even, s_odd, m_even, m_odd, acc_ref):
    n_tiles = q_ref.shape[2]
    seq, tq = s_even.shape
    n_chunks = seq // KV_CHUNK

    def score_chunk(q, kk_ref, s_w, c, m_run):
        rows = slice(c * KV_CHUNK, (c + 1) * KV_CHUNK)
        s = _dot(kk_ref[0, 0, rows, :], q)
        s_w[rows, :] = s
        mc = jnp.max(s.reshape(KV_CHUNK // F32_SUBLANES, F32_SUBLANES, tq), axis=0)
        return mc if m_run is None else jnp.maximum(m_run, mc)

    def reduce_chunk(s_r, m, c, acc):
        rows = slice(c * KV_CHUNK, (c + 1) * KV_CHUNK)
        p = jnp.exp2(s_r[rows, :] - m).astype(_bf16)
        part = _dot(vT_ref[0, 0, :, rows], p)
        return part if acc is None else acc + part

    s_bufs, m_bufs = (s_even, s_odd), (m_even, m_odd)

    def col_max(m_run):
        return jnp.max(m_run, axis=0, keepdims=True)

    def finish(acc, t):
        o_ref[0, 0, t] = acc[0:V_DIM] * (1.0 / acc[V_DIM:V_DIM + 1])

    @pl.when(pl.program_id(0) == 0)
    def _():
        m_run = None
        for c in range(n_chunks):
            m_run = score_chunk(q_ref[0, 0, 0], k_ref, s_even, c, m_run)
        m_even[...] = col_max(m_run)

    def run_tiles(j_lo, j_hi):
        m_cur = m_bufs[j_lo % 2][...]
        if j_lo >= 1:
            m_prev, acc_prev = m_bufs[(j_lo - 1) % 2][...], acc_ref[...]
        for j in range(j_lo, j_hi):
            if j + 1 < n_tiles:
                q, kk_ref = q_ref[0, 0, j + 1], k_ref
            else:
                q, kk_ref = qn_ref[0, 0, 0], kn_ref
            s_w, s_r = s_bufs[(j + 1) % 2], s_bufs[j % 2]
            m_run, acc = None, None
            for c in range(n_chunks):
                if c == 0 and j >= 1:
                    finish(reduce_chunk(s_w, m_prev, n_chunks - 1, acc_prev), j - 1)
                m_run = score_chunk(q, kk_ref, s_w, c, m_run)
                if c >= 1:
                    acc = reduce_chunk(s_r, m_cur, c - 1, acc)
            m_prev, m_cur, acc_prev = m_cur, col_max(m_run), acc
        if j_hi == n_tiles:
            finish(reduce_chunk(s_bufs[(j_hi - 1) % 2], m_prev, n_chunks - 1, acc_prev), j_hi - 1)
        else:
            m_bufs[(j_hi - 1) % 2][...] = m_prev
            acc_ref[...] = acc_prev
        m_bufs[j_hi % 2][...] = m_cur

    def first_half(_, carry):
        run_tiles(0, n_tiles // 2)
        return carry

    lax.fori_loop(0, 1 + (pl.program_id(0) >> 30), first_half, 0)
    run_tiles(n_tiles // 2, n_tiles)


def _kv_head(h):
    return jnp.where(h < MLA_HEADS, h, MLA_HEADS + (h - MLA_HEADS) // GQA_GROUP)


def _attention(qT, k, vT):
    bsz, _, n_tiles, _, tq = qT.shape
    seq = n_tiles * tq
    assert n_tiles % 2 == 0
    n_steps = bsz * N_Q_HEADS

    def cur(g):
        return g // N_Q_HEADS, g % N_Q_HEADS

    def nxt(g):
        return cur(jnp.minimum(g + 1, n_steps - 1))

    def q_map(bh):
        return lambda g: (*bh(g), 0, 0, 0)

    def kv_map(bh):
        def index(g):
            b, h = bh(g)
            return b, _kv_head(h), 0, 0
        return index

    return pl.pallas_call(
        _attn_kernel,
        grid=(n_steps,),
        in_specs=[pl.BlockSpec((1, 1, n_tiles, HEAD_PAD, tq), q_map(cur)),
                  pl.BlockSpec((1, 1, 1, HEAD_PAD, tq), q_map(nxt)),
                  pl.BlockSpec((1, 1, seq, HEAD_PAD), kv_map(cur)),
                  pl.BlockSpec((1, 1, seq, HEAD_PAD), kv_map(nxt)),
                  pl.BlockSpec((1, 1, V_ROWS, seq), kv_map(cur))],
        out_specs=pl.BlockSpec((1, 1, n_tiles, V_DIM, tq), q_map(cur)),
        out_shape=jax.ShapeDtypeStruct((bsz, N_Q_HEADS, n_tiles, V_DIM, tq), _f32),
        scratch_shapes=[pltpu.VMEM((seq, tq), _f32), pltpu.VMEM((seq, tq), _f32),
                        pltpu.VMEM((1, tq), _f32), pltpu.VMEM((1, tq), _f32),
                        pltpu.VMEM((V_ROWS, tq), _f32)],
        compiler_params=pltpu.CompilerParams(dimension_semantics=("arbitrary",),
                                             vmem_limit_bytes=VMEM_LIMIT_BYTES),
        name="attn",
    )(qT, qT, k, k, vT)


def _post_kernel(x_ref, o_ref_in, mod_ref, gom_ref, gog_ref, wout_ref, g2_ref, wgu_ref, wd_ref, gf_ref, out_ref,
                 *, final_norm):
    tm = x_ref.shape[0]
    x = x_ref[...]
    oT = jnp.concatenate([o_ref_in[0, :, t].reshape(N_Q_HEADS * V_DIM, TQ) for t in range(tm // TQ)],
                         axis=1)
    half = MLA_HEADS * MLA_V_DIM
    on = jnp.concatenate([_rms_fm(oT[:half], gom_ref[...]), _rms_fm(oT[half:], gog_ref[...])], axis=0)
    y = _dot_tn(on.astype(_bf16), wout_ref[...])
    x = x + mod_ref[0, 5:6, :] * y
    h = _norm_mod(x, g2_ref[...], mod_ref[0, 6:7, :], mod_ref[0, 7:8, :]).astype(_bf16)
    x = x + (0.5 * mod_ref[0, 8:9, :]) * _swiglu(h, wgu_ref, wd_ref)
    if final_norm:
        r = lax.rsqrt(jnp.mean(x * x, axis=-1, keepdims=True) + EPS)
        x = x * r * gf_ref[...]
    out_ref[...] = x


def _post(x2d, oT, mod3, gom, gog, wout, g2, wgu, wd, gf, seq, final_norm):
    n, d = x2d.shape
    tm = TM_POST
    tpb = seq // tm
    return pl.pallas_call(
        functools.partial(_post_kernel, final_norm=final_norm),
        grid=(n // tm,),
        in_specs=[pl.BlockSpec((tm, d), lambda i: (i, 0)),
                  pl.BlockSpec((1, N_Q_HEADS, tm // TQ, V_DIM, TQ), lambda i: (i // tpb, 0, i % tpb, 0, 0)),
                  pl.BlockSpec((1, N_MOD, d), lambda i: (i // tpb, 0, 0)),
                  _const_spec(gom.shape),
                  _const_spec(gog.shape),
                  _const_spec(wout.shape),
                  _const_spec((1, d)),
                  _const_spec(wgu.shape),
                  _const_spec(wd.shape),
                  _const_spec((1, d))],
        out_specs=pl.BlockSpec((tm, d), lambda i: (i, 0)),
        out_shape=jax.ShapeDtypeStruct((n, d), _f32),
        compiler_params=pltpu.CompilerParams(vmem_limit_bytes=VMEM_LIMIT_BYTES),
        name="post",
    )(x2d, oT, mod3, gom, gog, wout, g2, wgu, wd, gf)


def _rope_table(seq, dim):
    rows = seq // GRID_W
    row = jnp.repeat(jnp.arange(rows), GRID_W).astype(_f32)
    col = jnp.tile(jnp.arange(GRID_W), rows).astype(_f32)
    axis_dim = dim // 2
    inv_freq = ROPE_THETA ** (-(jnp.arange(axis_dim // 2, dtype=_f32) * 2.0 / axis_dim))
    ang_row = row[:, None] * inv_freq[None, :]
    ang_col = col[:, None] * inv_freq[None, :]
    return jnp.concatenate([jnp.cos(ang_row), jnp.sin(ang_row), jnp.cos(ang_col), jnp.sin(ang_col)], axis=1).T


def kernel(x, c, w_ada, b_ada, g_ffn1, w1_gu, w1_down, g_mix, w_in, g_q_lat, w_uq, g_kv_lat, w_ukv,
           g_qhead, g_khead, g_out_mla, g_out_gqa, w_out, g_ffn2, w2_gu, w2_down, g_final):
    bsz, seq, d = x.shape
    depth = w_ada.shape[0]
    t64 = _rope_table(seq, GQA_HEAD_DIM)
    t32 = _rope_table(seq, MLA_ROPE_DIM)
    x2d = x.reshape(bsz * seq, d)
    for l in range(depth):
        mod3 = _ada(c, w_ada[l], b_ada[l]).reshape(bsz, N_MOD, d)

        x2d, w2_gu_bf, w2_down_bf = _ffn(x2d, mod3, g_ffn1[l].reshape(1, d), w1_gu[l], w1_down[l], seq, 0,
                                         w2_gu[l], w2_down[l])

        winT = w_in[l].T.astype(_bf16)
        wuq = w_uq[l].reshape(Q_LORA_RANK, MLA_HEADS, MLA_NOPE_DIM + MLA_ROPE_DIM)
        wuqT = jnp.concatenate([wuq[:, :, :MLA_NOPE_DIM].reshape(Q_LORA_RANK, -1),
                                wuq[:, :, MLA_NOPE_DIM:].reshape(Q_LORA_RANK, -1)], axis=1).T.astype(_bf16)
        wukv = w_ukv[l].reshape(KV_LORA_RANK, MLA_HEADS, MLA_NOPE_DIM + MLA_V_DIM)
        wukvT = jnp.concatenate([wukv[:, :, :MLA_NOPE_DIM].reshape(KV_LORA_RANK, -1),
                                 wukv[:, :, MLA_NOPE_DIM:].reshape(KV_LORA_RANK, -1)], axis=1).T.astype(_bf16)
        qT, k, vT = _mix_pre(x2d, mod3, g_mix[l].reshape(1, d), winT,
                             g_q_lat[l].reshape(-1, 1), wuqT, g_kv_lat[l].reshape(-1, 1), wukvT,
                             g_qhead[l].reshape(-1, 1), g_khead[l].reshape(-1, 1), t64, t32, bsz, seq)

        oT = _attention(qT, k, vT)

        x2d = _post(x2d, oT, mod3, g_out_mla[l].reshape(-1, 1), g_out_gqa[l].reshape(-1, 1),
                    w_out[l].astype(_bf16), g_ffn2[l].reshape(1, d),
                    w2_gu_bf, w2_down_bf, g_final.reshape(1, d), seq,
                    final_norm=(l == depth - 1))
    return x2d.reshape(bsz, seq, d)
```

```python
import functools
import math

import jax
import jax.numpy as jnp
import numpy as np
from jax import lax
from jax.experimental import pallas as pl
from jax.experimental.pallas import tpu as pltpu

D_MODEL = 1024
GRID_W = 64
ROPE_THETA = 10000.0
EPS = 1e-6

MLA_HEADS = 8
MLA_NOPE_DIM = 64
MLA_ROPE_DIM = 32
MLA_V_DIM = 64
Q_LORA_RANK = 256
KV_LORA_RANK = 128

GQA_HEADS = 8
GQA_KV_HEADS = 2
GQA_HEAD_DIM = 64
GQA_GROUP = GQA_HEADS // GQA_KV_HEADS

D_FF = 2816
N_MOD = 9

N_Q_HEADS = MLA_HEADS + GQA_HEADS
N_KV_HEADS = MLA_HEADS + GQA_KV_HEADS
HEAD_PAD = 128
V_DIM = 64
F32_SUBLANES = 8
BF16_SUBLANES = 16
V_ROWS = V_DIM + BF16_SUBLANES
LOG2E = math.log2(math.e)

_OFF = np.cumsum([0, Q_LORA_RANK, KV_LORA_RANK, MLA_ROPE_DIM,
                  GQA_HEADS * GQA_HEAD_DIM, GQA_KV_HEADS * GQA_HEAD_DIM, GQA_KV_HEADS * GQA_HEAD_DIM])
D_IN = int(_OFF[-1])

VMEM_LIMIT_BYTES = 60 * 1024 * 1024

ADA_COLS = 1152
TM_FFN = 1024
TQ = 512
TM_MIX = 1024
TM_POST = 1024
MXU_TILE = 256
FF_CHUNK_TILES = (6, 5)
KV_CHUNK = MXU_TILE
ATTN_LAG = 2

_bf16 = jnp.bfloat16
_f32 = jnp.float32


def _dot(a, b):
    return jnp.dot(a, b, preferred_element_type=_f32)


def _dot_nt(a, b):
    return lax.dot_general(a, b, (((1,), (1,)), ((), ())), preferred_element_type=_f32)


def _dot_tn(a, b):
    return lax.dot_general(a, b, (((0,), (0,)), ((), ())), preferred_element_type=_f32)


def _const_spec(shape):
    return pl.BlockSpec(shape, lambda *_: (0,) * len(shape), pipeline_mode=pl.Buffered(1))


def _ada_kernel(c_ref, w_ref, b_ref, o_ref):
    c = c_ref[...]
    ca = c * jax.nn.sigmoid(c)
    w = w_ref[...]
    ca_hi = ca.astype(_bf16)
    ca_lo = (ca - ca_hi.astype(_f32)).astype(_bf16)
    w_hi = w.astype(_bf16)
    w_lo = (w - w_hi.astype(_f32)).astype(_bf16)
    o_ref[...] = _dot(ca_hi, w_hi) + (_dot(ca_hi, w_lo) + _dot(ca_lo, w_hi)) + b_ref[...]


def _ada(c, w_ada, b_ada):
    bsz, d = c.shape
    n = w_ada.shape[1]
    tn = ADA_COLS
    assert n % tn == 0
    return pl.pallas_call(
        _ada_kernel,
        grid=(n // tn,),
        in_specs=[pl.BlockSpec((bsz, d), lambda j: (0, 0)),
                  pl.BlockSpec((d, tn), lambda j: (0, j)),
                  pl.BlockSpec((1, tn), lambda j: (0, j))],
        out_specs=pl.BlockSpec((bsz, tn), lambda j: (0, j)),
        out_shape=jax.ShapeDtypeStruct((bsz, n), _f32),
        compiler_params=pltpu.CompilerParams(vmem_limit_bytes=VMEM_LIMIT_BYTES),
        name="ada",
    )(c, w_ada, b_ada.reshape(1, n))


def _norm_mod(x, g, shift, scale):
    r = lax.rsqrt(jnp.mean(x * x, axis=-1, keepdims=True) + EPS)
    return (x * r) * (g * (1.0 + scale)) + shift


def _swiglu(h, wgu_ref, wd_ref):
    acc = None
    lo = 0
    for tiles in FF_CHUNK_TILES:
        hi = lo + tiles * MXU_TILE
        a = _dot(h, wgu_ref[:, lo:hi])
        b = _dot(h, wgu_ref[:, D_FF + lo:D_FF + hi])
        act = (a * jax.nn.sigmoid(a) * b).astype(_bf16)
        part = _dot(act, wd_ref[lo:hi, :])
        acc = part if acc is None else acc + part
        lo = hi
    assert lo == D_FF
    return acc


def _ffn_kernel(x_ref, mod_ref, g_ref, wgu_ref, wd_ref, wa_ref, wb_ref, o_ref, wa_out, wb_out, *, mod_base):
    shift = mod_ref[0, mod_base:mod_base + 1, :]
    scale = mod_ref[0, mod_base + 1:mod_base + 2, :]
    gate = mod_ref[0, mod_base + 2:mod_base + 3, :]
    x = x_ref[...]
    h = _norm_mod(x, g_ref[...], shift, scale).astype(_bf16)
    o_ref[...] = x + (0.5 * gate) * _swiglu(h, wgu_ref, wd_ref)
    wa_out[...] = wa_ref[...].astype(_bf16)
    wb_out[...] = wb_ref[...].astype(_bf16)


def _ffn(x2d, mod3, g, wgu, wd, seq, mod_base, cast_a, cast_b):
    n, d = x2d.shape
    tm = TM_FFN
    tpb = seq // tm
    n_steps = n // tm

    def cast_spec(w):
        rows = w.shape[0]
        visits = 1
        while rows % (n_steps // visits) or (rows // (n_steps // visits)) % BF16_SUBLANES:
            visits *= 2
            assert visits <= n_steps
        return pl.BlockSpec((rows // (n_steps // visits), w.shape[1]), lambda i: (i // visits, 0))

    return pl.pallas_call(
        functools.partial(_ffn_kernel, mod_base=mod_base),
        grid=(n_steps,),
        in_specs=[pl.BlockSpec((tm, d), lambda i: (i, 0)),
                  pl.BlockSpec((1, N_MOD, d), lambda i: (i // tpb, 0, 0)),
                  _const_spec((1, d)),
                  _const_spec(wgu.shape),
                  _const_spec(wd.shape),
                  cast_spec(cast_a),
                  cast_spec(cast_b)],
        out_specs=(pl.BlockSpec((tm, d), lambda i: (i, 0)), cast_spec(cast_a), cast_spec(cast_b)),
        out_shape=(jax.ShapeDtypeStruct((n, d), _f32),
                   jax.ShapeDtypeStruct(cast_a.shape, _bf16),
                   jax.ShapeDtypeStruct(cast_b.shape, _bf16)),
        compiler_params=pltpu.CompilerParams(vmem_limit_bytes=VMEM_LIMIT_BYTES),
        name="ffn1",
    )(x2d, mod3, g, wgu, wd, cast_a, cast_b)


def _rope_fm(x, tab):
    q = x.shape[0] // 4
    x1r, x2r, x1c, x2c = x[0:q], x[q:2 * q], x[2 * q:3 * q], x[3 * q:4 * q]
    cr, sr, cc, sc = tab[0:q], tab[q:2 * q], tab[2 * q:3 * q], tab[3 * q:4 * q]
    return jnp.concatenate([x1r * cr - x2r * sr, x1r * sr + x2r * cr,
                            x1c * cc - x2c * sc, x1c * sc + x2c * cc], axis=0)


def _rms_fm(x, g_col, extra=1.0):
    r = lax.rsqrt(jnp.mean(x * x, axis=0, keepdims=True) + EPS)
    if extra != 1.0:
        r = r * extra
    return x * r * g_col


def _mix_pre_kernel(x_ref, mod_ref, g_ref, winT_ref, gq_ref, wuqT_ref, gkv_ref, wukvT_ref,
                    gqh_ref, gkh_ref, t64_ref, t32_ref, qT_ref, k_ref, vT_ref):
    tm = x_ref.shape[0]
    x = x_ref[...]
    h = _norm_mod(x, g_ref[...], mod_ref[0, 3:4, :], mod_ref[0, 4:5, :]).astype(_bf16)
    zT = _dot_nt(winT_ref[...], h)
    q_lat = zT[_OFF[0]:_OFF[1]]
    kv_lat = zT[_OFF[1]:_OFF[2]]
    k_rope = zT[_OFF[2]:_OFF[3]]
    q_g = zT[_OFF[3]:_OFF[4]]
    k_g = zT[_OFF[4]:_OFF[5]]
    v_g = zT[_OFF[5]:_OFF[6]]
    t64 = t64_ref[...]
    t32 = t32_ref[...]

    zeros32 = jnp.zeros((HEAD_PAD - MLA_NOPE_DIM - MLA_ROPE_DIM, tm), _f32)
    zeros64 = jnp.zeros((HEAD_PAD - GQA_HEAD_DIM, tm), _f32)
    ones_pad = (lax.broadcasted_iota(jnp.int32, (V_ROWS - V_DIM, tm), 0) == 0).astype(_f32)

    def put_q(head, qh):
        for t in range(tm // TQ):
            qT_ref[0, head, t] = qh[:, t * TQ:(t + 1) * TQ].astype(_bf16)

    mla_scale = (MLA_NOPE_DIM + MLA_ROPE_DIM) ** -0.5 * LOG2E
    qn = _rms_fm(q_lat, gq_ref[...], mla_scale).astype(_bf16)
    qa = _dot(wuqT_ref[...], qn)
    nope_rows = MLA_HEADS * MLA_NOPE_DIM
    for hd in range(MLA_HEADS):
        nope = qa[hd * MLA_NOPE_DIM:(hd + 1) * MLA_NOPE_DIM]
        pe = _rope_fm(qa[nope_rows + hd * MLA_ROPE_DIM:nope_rows + (hd + 1) * MLA_ROPE_DIM], t32)
        put_q(hd, jnp.concatenate([nope, pe, zeros32], axis=0))

    kvn = _rms_fm(kv_lat, gkv_ref[...]).astype(_bf16)
    kva = _dot(wukvT_ref[...], kvn)
    k_pe = _rope_fm(k_rope, t32)
    for hd in range(MLA_HEADS):
        k_nope = kva[hd * MLA_NOPE_DIM:(hd + 1) * MLA_NOPE_DIM]
        kT = jnp.concatenate([k_nope, k_pe, zeros32], axis=0)
        k_ref[0, hd] = kT.T.astype(_bf16)
        v = kva[nope_rows + hd * MLA_V_DIM:nope_rows + (hd + 1) * MLA_V_DIM]
        vT_ref[0, hd] = jnp.concatenate([v, ones_pad], axis=0).astype(_bf16)

    gqa_scale = GQA_HEAD_DIM ** -0.5 * LOG2E
    for hd in range(GQA_HEADS):
        xh = _rms_fm(q_g[hd * GQA_HEAD_DIM:(hd + 1) * GQA_HEAD_DIM], gqh_ref[...], gqa_scale)
        put_q(MLA_HEADS + hd, jnp.concatenate([_rope_fm(xh, t64), zeros64], axis=0))
    for hd in range(GQA_KV_HEADS):
        xh = _rms_fm(k_g[hd * GQA_HEAD_DIM:(hd + 1) * GQA_HEAD_DIM], gkh_ref[...])
        kT = jnp.concatenate([_rope_fm(xh, t64), zeros64], axis=0)
        k_ref[0, MLA_HEADS + hd] = kT.T.astype(_bf16)
        v = v_g[hd * GQA_HEAD_DIM:(hd + 1) * GQA_HEAD_DIM]
        vT_ref[0, MLA_HEADS + hd] = jnp.concatenate([v, ones_pad], axis=0).astype(_bf16)


def _mix_pre(x2d, mod3, g, winT, gq, wuqT, gkv, wukvT, gqh, gkh, t64, t32, bsz, seq):
    n, d = x2d.shape
    tm = TM_MIX
    tpb = seq // tm
    qpt = tm // TQ
    out_shape = (jax.ShapeDtypeStruct((bsz, N_Q_HEADS, seq // TQ, HEAD_PAD, TQ), _bf16),
                 jax.ShapeDtypeStruct((bsz, N_KV_HEADS, seq, HEAD_PAD), _bf16),
                 jax.ShapeDtypeStruct((bsz, N_KV_HEADS, V_ROWS, seq), _bf16))
    return pl.pallas_call(
        _mix_pre_kernel,
        grid=(n // tm,),
        in_specs=[pl.BlockSpec((tm, d), lambda i: (i, 0)),
                  pl.BlockSpec((1, N_MOD, d), lambda i: (i // tpb, 0, 0)),
                  _const_spec((1, d)),
                  _const_spec(winT.shape),
                  _const_spec(gq.shape),
                  _const_spec(wuqT.shape),
                  _const_spec(gkv.shape),
                  _const_spec(wukvT.shape),
                  _const_spec(gqh.shape),
                  _const_spec(gkh.shape),
                  pl.BlockSpec((GQA_HEAD_DIM, tm), lambda i: (0, i % tpb)),
                  pl.BlockSpec((MLA_ROPE_DIM, tm), lambda i: (0, i % tpb))],
        out_specs=(pl.BlockSpec((1, N_Q_HEADS, qpt, HEAD_PAD, TQ), lambda i: (i // tpb, 0, i % tpb, 0, 0)),
                   pl.BlockSpec((1, N_KV_HEADS, tm, HEAD_PAD), lambda i: (i // tpb, 0, i % tpb, 0)),
                   pl.BlockSpec((1, N_KV_HEADS, V_ROWS, tm), lambda i: (i // tpb, 0, 0, i % tpb))),
        out_shape=out_shape,
        compiler_params=pltpu.CompilerParams(vmem_limit_bytes=VMEM_LIMIT_BYTES),
        name="mix_pre",
    )(x2d, mod3, g, winT, gq, wuqT, gkv, wukvT, gqh, gkh, t64, t32)


def _attn_kernel(q_ref, k_ref, vT_ref, o_ref, ring, mc_ring, acc_ref, m_ref):
    n_tiles = q_ref.shape[2]
    n_slots, chunk, tq = ring.shape
    n_chunks = k_ref.shape[2] // chunk
    n_pairs = n_tiles * n_chunks

    def score(p):
        j, c = divmod(p, n_chunks)
        s = _dot(k_ref[0, 0, c * chunk:(c + 1) * chunk, :], q_ref[0, 0, j])
        ring[p % n_slots] = s
        mc_ring[p % n_slots] = jnp.max(s.reshape(chunk // F32_SUBLANES, F32_SUBLANES, tq), axis=0)

    def fold(p, m_old, acc):
        j, c = divmod(p, n_chunks)
        s = ring[p % n_slots]
        mc = jnp.max(mc_ring[p % n_slots], axis=0, keepdims=True)
        v = vT_ref[0, 0, :, c * chunk:(c + 1) * chunk]
        if c == 0:
            m_new = mc
            acc = _dot(v, jnp.exp2(s - m_new).astype(_bf16))
        else:
            m_new = jnp.maximum(m_old, mc)
            acc = acc * jnp.exp2(m_old - m_new) + _dot(v, jnp.exp2(s - m_new).astype(_bf16))
        if c == n_chunks - 1:
            o_ref[0, 0, j] = acc[0:V_DIM] * (1.0 / acc[V_DIM:V_DIM + 1])
        return m_new, acc

    def run(p_lo, p_hi):
        first_fold = p_lo - ATTN_LAG
        if first_fold > 0 and first_fold % n_chunks:
            m_old, acc = m_ref[...], acc_ref[...]
        else:
            m_old, acc = None, None
        for p in range(p_lo, p_hi):
            if p < n_pairs:
                score(p)
            if p >= ATTN_LAG:
                m_old, acc = fold(p - ATTN_LAG, m_old, acc)
        if (p_hi - ATTN_LAG) % n_chunks:
            m_ref[...] = m_old
            acc_ref[...] = acc

    half = n_pairs // 2

    def first_half(_, carry):
        run(0, half)
        return carry

    lax.fori_loop(0, 1 + (pl.program_id(0) >> 30), first_half, 0)
    run(half, n_pairs + ATTN_LAG)


def _kv_head(h):
    return jnp.where(h < MLA_HEADS, h, MLA_HEADS + (h - MLA_HEADS) // GQA_GROUP)


def _attention(qT, k, vT):
    bsz, _, n_tiles, _, tq = qT.shape
    seq = n_tiles * tq
    assert n_tiles % 2 == 0 and seq % KV_CHUNK == 0
    return pl.pallas_call(
        _attn_kernel,
        grid=(bsz, N_Q_HEADS),
        in_specs=[pl.BlockSpec((1, 1, n_tiles, HEAD_PAD, tq), lambda b, h: (b, h, 0, 0, 0)),
                  pl.BlockSpec((1, 1, seq, HEAD_PAD), lambda b, h: (b, _kv_head(h), 0, 0)),
                  pl.BlockSpec((1, 1, V_ROWS, seq), lambda b, h: (b, _kv_head(h), 0, 0))],
        out_specs=pl.BlockSpec((1, 1, n_tiles, V_DIM, tq), lambda b, h: (b, h, 0, 0, 0)),
        out_shape=jax.ShapeDtypeStruct((bsz, N_Q_HEADS, n_tiles, V_DIM, tq), _f32),
        scratch_shapes=[pltpu.VMEM((ATTN_LAG + 1, KV_CHUNK, tq), _f32),
                        pltpu.VMEM((ATTN_LAG + 1, F32_SUBLANES, tq), _f32),
                        pltpu.VMEM((V_ROWS, tq), _f32), pltpu.VMEM((1, tq), _f32)],
        compiler_params=pltpu.CompilerParams(vmem_limit_bytes=VMEM_LIMIT_BYTES),
        name="attn",
    )(qT, k, vT)


def _post_kernel(x_ref, o_ref_in, mod_ref, gom_ref, gog_ref, wout_ref, g2_ref, wgu_ref, wd_ref, gf_ref, out_ref,
                 *, final_norm):
    tm = x_ref.shape[0]
    x = x_ref[...]
    oT = jnp.concatenate([o_ref_in[0, :, t].reshape(N_Q_HEADS * V_DIM, TQ) for t in range(tm // TQ)],
                         axis=1)
    half = MLA_HEADS * MLA_V_DIM
    on = jnp.concatenate([_rms_fm(oT[:half], gom_ref[...]), _rms_fm(oT[half:], gog_ref[...])], axis=0)
    y = _dot_tn(on.astype(_bf16), wout_ref[...])
    x = x + mod_ref[0, 5:6, :] * y
    h = _norm_mod(x, g2_ref[...], mod_ref[0, 6:7, :], mod_ref[0, 7:8, :]).astype(_bf16)
    x = x + (0.5 * mod_ref[0, 8:9, :]) * _swiglu(h, wgu_ref, wd_ref)
    if final_norm:
        r = lax.rsqrt(jnp.mean(x * x, axis=-1, keepdims=True) + EPS)
        x = x * r * gf_ref[...]
    out_ref[...] = x


def _post(x2d, oT, mod3, gom, gog, wout, g2, wgu, wd, gf, seq, final_norm):
    n, d = x2d.shape
    tm = TM_POST
    tpb = seq // tm
    return pl.pallas_call(
        functools.partial(_post_kernel, final_norm=final_norm),
        grid=(n // tm,),
        in_specs=[pl.BlockSpec((tm, d), lambda i: (i, 0)),
                  pl.BlockSpec((1, N_Q_HEADS, tm // TQ, V_DIM, TQ), lambda i: (i // tpb, 0, i % tpb, 0, 0)),
                  pl.BlockSpec((1, N_MOD, d), lambda i: (i // tpb, 0, 0)),
                  _const_spec(gom.shape),
                  _const_spec(gog.shape),
                  _const_spec(wout.shape),
                  _const_spec((1, d)),
                  _const_spec(wgu.shape),
                  _const_spec(wd.shape),
                  _const_spec((1, d))],
        out_specs=pl.BlockSpec((tm, d), lambda i: (i, 0)),
        out_shape=jax.ShapeDtypeStruct((n, d), _f32),
        compiler_params=pltpu.CompilerParams(vmem_limit_bytes=VMEM_LIMIT_BYTES),
        name="post",
    )(x2d, oT, mod3, gom, gog, wout, g2, wgu, wd, gf)


def _rope_table(seq, dim):
    rows = seq // GRID_W
    row = jnp.repeat(jnp.arange(rows), GRID_W).astype(_f32)
    col = jnp.tile(jnp.arange(GRID_W), rows).astype(_f32)
    axis_dim = dim // 2
    inv_freq = ROPE_THETA ** (-(jnp.arange(axis_dim // 2, dtype=_f32) * 2.0 / axis_dim))
    ang_row = row[:, None] * inv_freq[None, :]
    ang_col = col[:, None] * inv_freq[None, :]
    return jnp.concatenate([jnp.cos(ang_row), jnp.sin(ang_row), jnp.cos(ang_col), jnp.sin(ang_col)], axis=1).T


def kernel(x, c, w_ada, b_ada, g_ffn1, w1_gu, w1_down, g_mix, w_in, g_q_lat, w_uq, g_kv_lat, w_ukv,
           g_qhead, g_khead, g_out_mla, g_out_gqa, w_out, g_ffn2, w2_gu, w2_down, g_final):
    bsz, seq, d = x.shape
    depth = w_ada.shape[0]
    t64 = _rope_table(seq, GQA_HEAD_DIM)
    t32 = _rope_table(seq, MLA_ROPE_DIM)
    x2d = x.reshape(bsz * seq, d)
    for l in range(depth):
        mod3 = _ada(c, w_ada[l], b_ada[l]).reshape(bsz, N_MOD, d)

        x2d, w2_gu_bf, w2_down_bf = _ffn(x2d, mod3, g_ffn1[l].reshape(1, d), w1_gu[l].astype(_bf16),
                                         w1_down[l].astype(_bf16), seq, 0, w2_gu[l], w2_down[l])

        winT = w_in[l].T.astype(_bf16)
        wuq = w_uq[l].reshape(Q_LORA_RANK, MLA_HEADS, MLA_NOPE_DIM + MLA_ROPE_DIM)
        wuqT = jnp.concatenate([wuq[:, :, :MLA_NOPE_DIM].reshape(Q_LORA_RANK, -1),
                                wuq[:, :, MLA_NOPE_DIM:].reshape(Q_LORA_RANK, -1)], axis=1).T.astype(_bf16)
        wukv = w_ukv[l].reshape(KV_LORA_RANK, MLA_HEADS, MLA_NOPE_DIM + MLA_V_DIM)
        wukvT = jnp.concatenate([wukv[:, :, :MLA_NOPE_DIM].reshape(KV_LORA_RANK, -1),
                                 wukv[:, :, MLA_NOPE_DIM:].reshape(KV_LORA_RANK, -1)], axis=1).T.astype(_bf16)
        qT, k, vT = _mix_pre(x2d, mod3, g_mix[l].reshape(1, d), winT,
                             g_q_lat[l].reshape(-1, 1), wuqT, g_kv_lat[l].reshape(-1, 1), wukvT,
                             g_qhead[l].reshape(-1, 1), g_khead[l].reshape(-1, 1), t64, t32, bsz, seq)

        oT = _attention(qT, k, vT)

        x2d = _post(x2d, oT, mod3, g_out_mla[l].reshape(-1, 1), g_out_gqa[l].reshape(-1, 1),
                    w_out[l].astype(_bf16), g_ffn2[l].reshape(1, d),
                    w2_gu_bf, w2_down_bf, g_final.reshape(1, d), seq,
                    final_norm=(l == depth - 1))
    return x2d.reshape(bsz, seq, d)
```

```python
import functools
import math

import jax
import jax.numpy as jnp
import numpy as np
from jax import lax
from jax.experimental import pallas as pl
from jax.experimental.pallas import tpu as pltpu

D_MODEL = 1024
GRID_W = 64
ROPE_THETA = 10000.0
EPS = 1e-6

MLA_HEADS = 8
MLA_NOPE_DIM = 64
MLA_ROPE_DIM = 32
MLA_V_DIM = 64
Q_LORA_RANK = 256
KV_LORA_RANK = 128

GQA_HEADS = 8
GQA_KV_HEADS = 2
GQA_HEAD_DIM = 64
GQA_GROUP = GQA_HEADS // GQA_KV_HEADS

D_FF = 2816
N_MOD = 9

N_Q_HEADS = MLA_HEADS + GQA_HEADS
N_KV_HEADS = MLA_HEADS + GQA_KV_HEADS
HEAD_PAD = 128
V_DIM = 64
F32_SUBLANES = 8
BF16_SUBLANES = 16
V_ROWS = V_DIM + BF16_SUBLANES
LOG2E = math.log2(math.e)

_OFF = np.cumsum([0, Q_LORA_RANK, KV_LORA_RANK, MLA_ROPE_DIM,
                  GQA_HEADS * GQA_HEAD_DIM, GQA_KV_HEADS * GQA_HEAD_DIM, GQA_KV_HEADS * GQA_HEAD_DIM])
D_IN = int(_OFF[-1])

VMEM_LIMIT_BYTES = 60 * 1024 * 1024

ADA_COLS = 1152
TM_FFN = 1024
TQ = 512
TM_MIX = 1024
TM_POST = 1024
MXU_TILE = 256
FF_CHUNK_TILES = (6, 5)
KV_CHUNK = MXU_TILE

_bf16 = jnp.bfloat16
_f32 = jnp.float32


def _dot(a, b):
    return jnp.dot(a, b, preferred_element_type=_f32)


def _dot_nt(a, b):
    return lax.dot_general(a, b, (((1,), (1,)), ((), ())), preferred_element_type=_f32)


def _dot_tn(a, b):
    return lax.dot_general(a, b, (((0,), (0,)), ((), ())), preferred_element_type=_f32)


def _const_spec(shape):
    return pl.BlockSpec(shape, lambda *_: (0,) * len(shape), pipeline_mode=pl.Buffered(1))


def _ada_kernel(c_ref, w_ref, b_ref, o_ref):
    c = c_ref[...]
    ca = c * jax.nn.sigmoid(c)
    w = w_ref[...]
    ca_hi = ca.astype(_bf16)
    ca_lo = (ca - ca_hi.astype(_f32)).astype(_bf16)
    w_hi = w.astype(_bf16)
    w_lo = (w - w_hi.astype(_f32)).astype(_bf16)
    o_ref[...] = _dot(ca_hi, w_hi) + (_dot(ca_hi, w_lo) + _dot(ca_lo, w_hi)) + b_ref[...]


def _ada(c, w_ada, b_ada):
    bsz, d = c.shape
    n = w_ada.shape[1]
    tn = ADA_COLS
    assert n % tn == 0
    return pl.pallas_call(
        _ada_kernel,
        grid=(n // tn,),
        in_specs=[pl.BlockSpec((bsz, d), lambda j: (0, 0)),
                  pl.BlockSpec((d, tn), lambda j: (0, j)),
                  pl.BlockSpec((1, tn), lambda j: (0, j))],
        out_specs=pl.BlockSpec((bsz, tn), lambda j: (0, j)),
        out_shape=jax.ShapeDtypeStruct((bsz, n), _f32),
        compiler_params=pltpu.CompilerParams(vmem_limit_bytes=VMEM_LIMIT_BYTES),
        name="ada",
    )(c, w_ada, b_ada.reshape(1, n))


def _norm_mod(x, g, shift, scale):
    r = lax.rsqrt(jnp.mean(x * x, axis=-1, keepdims=True) + EPS)
    return (x * r) * (g * (1.0 + scale)) + shift


def _swiglu(h, wgu_ref, wd_ref):
    acc = None
    lo = 0
    for tiles in FF_CHUNK_TILES:
        hi = lo + tiles * MXU_TILE
        a = _dot(h, wgu_ref[:, lo:hi])
        b = _dot(h, wgu_ref[:, D_FF + lo:D_FF + hi])
        act = (a * jax.nn.sigmoid(a) * b).astype(_bf16)
        part = _dot(act, wd_ref[lo:hi, :])
        acc = part if acc is None else acc + part
        lo = hi
    assert lo == D_FF
    return acc


def _ffn_kernel(x_ref, mod_ref, g_ref, wgu_ref, wd_ref, wa_ref, wb_ref, o_ref, wa_out, wb_out, *, mod_base):
    shift = mod_ref[0, mod_base:mod_base + 1, :]
    scale = mod_ref[0, mod_base + 1:mod_base + 2, :]
    gate = mod_ref[0, mod_base + 2:mod_base + 3, :]
    x = x_ref[...]
    h = _norm_mod(x, g_ref[...], shift, scale).astype(_bf16)
    o_ref[...] = x + (0.5 * gate) * _swiglu(h, wgu_ref, wd_ref)
    wa_out[...] = wa_ref[...].astype(_bf16)
    wb_out[...] = wb_ref[...].astype(_bf16)


def _ffn(x2d, mod3, g, wgu, wd, seq, mod_base, cast_a, cast_b):
    n, d = x2d.shape
    tm = TM_FFN
    tpb = seq // tm
    n_steps = n // tm

    def cast_spec(w):
        rows = w.shape[0]
        visits = 1
        while rows % (n_steps // visits) or (rows // (n_steps // visits)) % BF16_SUBLANES:
            visits *= 2
            assert visits <= n_steps
        return pl.BlockSpec((rows // (n_steps // visits), w.shape[1]), lambda i: (i // visits, 0))

    return pl.pallas_call(
        functools.partial(_ffn_kernel, mod_base=mod_base),
        grid=(n_steps,),
        in_specs=[pl.BlockSpec((tm, d), lambda i: (i, 0)),
                  pl.BlockSpec((1, N_MOD, d), lambda i: (i // tpb, 0, 0)),
                  _const_spec((1, d)),
                  _const_spec(wgu.shape),
                  _const_spec(wd.shape),
                  cast_spec(cast_a),
                  cast_spec(cast_b)],
        out_specs=(pl.BlockSpec((tm, d), lambda i: (i, 0)), cast_spec(cast_a), cast_spec(cast_b)),
        out_shape=(jax.ShapeDtypeStruct((n, d), _f32),
                   jax.ShapeDtypeStruct(cast_a.shape, _bf16),
                   jax.ShapeDtypeStruct(cast_b.shape, _bf16)),
        compiler_params=pltpu.CompilerParams(vmem_limit_bytes=VMEM_LIMIT_BYTES),
        name="ffn1",
    )(x2d, mod3, g, wgu, wd, cast_a, cast_b)


def _rope_fm(x, tab):
    q = x.shape[0] // 4
    x1r, x2r, x1c, x2c = x[0:q], x[q:2 * q], x[2 * q:3 * q], x[3 * q:4 * q]
    cr, sr, cc, sc = tab[0:q], tab[q:2 * q], tab[2 * q:3 * q], tab[3 * q:4 * q]
    return jnp.concatenate([x1r * cr - x2r * sr, x1r * sr + x2r * cr,
                            x1c * cc - x2c * sc, x1c * sc + x2c * cc], axis=0)


def _rms_fm(x, g_col, extra=1.0):
    r = lax.rsqrt(jnp.mean(x * x, axis=0, keepdims=True) + EPS)
    if extra != 1.0:
        r = r * extra
    return x * r * g_col


def _mix_pre_kernel(x_ref, mod_ref, g_ref, winT_ref, gq_ref, wuqT_ref, gkv_ref, wukvT_ref,
                    gqh_ref, gkh_ref, t64_ref, t32_ref, qT_ref, k_ref, vT_ref):
    tm = x_ref.shape[0]
    x = x_ref[...]
    h = _norm_mod(x, g_ref[...], mod_ref[0, 3:4, :], mod_ref[0, 4:5, :]).astype(_bf16)
    zT = _dot_nt(winT_ref[...], h)
    q_lat = zT[_OFF[0]:_OFF[1]]
    kv_lat = zT[_OFF[1]:_OFF[2]]
    k_rope = zT[_OFF[2]:_OFF[3]]
    q_g = zT[_OFF[3]:_OFF[4]]
    k_g = zT[_OFF[4]:_OFF[5]]
    v_g = zT[_OFF[5]:_OFF[6]]
    t64 = t64_ref[...]
    t32 = t32_ref[...]

    zeros32 = jnp.zeros((HEAD_PAD - MLA_NOPE_DIM - MLA_ROPE_DIM, tm), _f32)
    zeros64 = jnp.zeros((HEAD_PAD - GQA_HEAD_DIM, tm), _f32)
    ones_pad = (lax.broadcasted_iota(jnp.int32, (V_ROWS - V_DIM, tm), 0) == 0).astype(_f32)

    def put_q(head, qh):
        for t in range(tm // TQ):
            qT_ref[0, head, t] = qh[:, t * TQ:(t + 1) * TQ].astype(_bf16)

    mla_scale = (MLA_NOPE_DIM + MLA_ROPE_DIM) ** -0.5 * LOG2E
    qn = _rms_fm(q_lat, gq_ref[...], mla_scale).astype(_bf16)
    qa = _dot(wuqT_ref[...], qn)
    nope_rows = MLA_HEADS * MLA_NOPE_DIM
    for hd in range(MLA_HEADS):
        nope = qa[hd * MLA_NOPE_DIM:(hd + 1) * MLA_NOPE_DIM]
        pe = _rope_fm(qa[nope_rows + hd * MLA_ROPE_DIM:nope_rows + (hd + 1) * MLA_ROPE_DIM], t32)
        put_q(hd, jnp.concatenate([nope, pe, zeros32], axis=0))

    kvn = _rms_fm(kv_lat, gkv_ref[...]).astype(_bf16)
    kva = _dot(wukvT_ref[...], kvn)
    k_pe = _rope_fm(k_rope, t32)
    for hd in range(MLA_HEADS):
        k_nope = kva[hd * MLA_NOPE_DIM:(hd + 1) * MLA_NOPE_DIM]
        kT = jnp.concatenate([k_nope, k_pe, zeros32], axis=0)
        k_ref[0, hd] = kT.T.astype(_bf16)
        v = kva[nope_rows + hd * MLA_V_DIM:nope_rows + (hd + 1) * MLA_V_DIM]
        vT_ref[0, hd] = jnp.concatenate([v, ones_pad], axis=0).astype(_bf16)

    gqa_scale = GQA_HEAD_DIM ** -0.5 * LOG2E
    for hd in range(GQA_HEADS):
        xh = _rms_fm(q_g[hd * GQA_HEAD_DIM:(hd + 1) * GQA_HEAD_DIM], gqh_ref[...], gqa_scale)
        put_q(MLA_HEADS + hd, jnp.concatenate([_rope_fm(xh, t64), zeros64], axis=0))
    for hd in range(GQA_KV_HEADS):
        xh = _rms_fm(k_g[hd * GQA_HEAD_DIM:(hd + 1) * GQA_HEAD_DIM], gkh_ref[...])
        kT = jnp.concatenate([_rope_fm(xh, t64), zeros64], axis=0)
        k_ref[0, MLA_HEADS + hd] = kT.T.astype(_bf16)
        v = v_g[hd * GQA_HEAD_DIM:(hd + 1) * GQA_HEAD_DIM]
        vT_ref[0, MLA_HEADS + hd] = jnp.concatenate([v, ones_pad], axis=0).astype(_bf16)


def _mix_pre(x2d, mod3, g, winT, gq, wuqT, gkv, wukvT, gqh, gkh, t64, t32, bsz, seq):
    n, d = x2d.shape
    tm = TM_MIX
    tpb = seq // tm
    qpt = tm // TQ
    out_shape = (jax.ShapeDtypeStruct((bsz, N_Q_HEADS, seq // TQ, HEAD_PAD, TQ), _bf16),
                 jax.ShapeDtypeStruct((bsz, N_KV_HEADS, seq, HEAD_PAD), _bf16),
                 jax.ShapeDtypeStruct((bsz, N_KV_HEADS, V_ROWS, seq), _bf16))
    return pl.pallas_call(
        _mix_pre_kernel,
        grid=(n // tm,),
        in_specs=[pl.BlockSpec((tm, d), lambda i: (i, 0)),
                  pl.BlockSpec((1, N_MOD, d), lambda i: (i // tpb, 0, 0)),
                  _const_spec((1, d)),
                  _const_spec(winT.shape),
                  _const_spec(gq.shape),
                  _const_spec(wuqT.shape),
                  _const_spec(gkv.shape),
                  _const_spec(wukvT.shape),
                  _const_spec(gqh.shape),
                  _const_spec(gkh.shape),
                  pl.BlockSpec((GQA_HEAD_DIM, tm), lambda i: (0, i % tpb)),
                  pl.BlockSpec((MLA_ROPE_DIM, tm), lambda i: (0, i % tpb))],
        out_specs=(pl.BlockSpec((1, N_Q_HEADS, qpt, HEAD_PAD, TQ), lambda i: (i // tpb, 0, i % tpb, 0, 0)),
                   pl.BlockSpec((1, N_KV_HEADS, tm, HEAD_PAD), lambda i: (i // tpb, 0, i % tpb, 0)),
                   pl.BlockSpec((1, N_KV_HEADS, V_ROWS, tm), lambda i: (i // tpb, 0, 0, i % tpb))),
        out_shape=out_shape,
        compiler_params=pltpu.CompilerParams(vmem_limit_bytes=VMEM_LIMIT_BYTES),
        name="mix_pre",
    )(x2d, mod3, g, winT, gq, wuqT, gkv, wukvT, gqh, gkh, t64, t32)


def _attn_kernel(q_ref, qn_ref, k_ref, kn_ref, vT_ref, o_ref, s_even, s_odd, m_even, m_odd, acc_ref):
    n_tiles = q_ref.shape[2]
    seq, tq = s_even.shape
    n_chunks = seq // KV_CHUNK

    def score_chunk(q, kk_ref, s_w, c, m_run):
        rows = slice(c * KV_CHUNK, (c + 1) * KV_CHUNK)
        s = _dot(kk_ref[0, 0, rows, :], q)
        s_w[rows, :] = s
        mc = jnp.max(s.reshape(KV_CHUNK // F32_SUBLANES, F32_SUBLANES, tq), axis=0)
        return mc if m_run is None else jnp.maximum(m_run, mc)

    def reduce_chunk(s_r, m, c, acc):
        rows = slice(c * KV_CHUNK, (c + 1) * KV_CHUNK)
        p = jnp.exp2(s_r[rows, :] - m).astype(_bf16)
        part = _dot(vT_ref[0, 0, :, rows], p)
        return part if acc is None else acc + part

    s_bufs, m_bufs = (s_even, s_odd), (m_even, m_odd)

    def col_max(m_run):
        return jnp.max(m_run, axis=0, keepdims=True)

    def finish(acc, t):
        o_ref[0, 0, t] = (acc[0:V_DIM] * (1.0 / acc[V_DIM:V_DIM + 1])).astype(o_ref.dtype)

    @pl.when(pl.program_id(0) == 0)
    def _():
        m_run = None
        for c in range(n_chunks):
            m_run = score_chunk(q_ref[0, 0, 0], k_ref, s_even, c, m_run)
        m_even[...] = col_max(m_run)

    def run_tiles(j_lo, j_hi):
        m_cur = m_bufs[j_lo % 2][...]
        if j_lo >= 1:
            m_prev, acc_prev = m_bufs[(j_lo - 1) % 2][...], acc_ref[...]
        for j in range(j_lo, j_hi):
            if j + 1 < n_tiles:
                q, kk_ref = q_ref[0, 0, j + 1], k_ref
            else:
                q, kk_ref = qn_ref[0, 0, 0], kn_ref
            s_w, s_r = s_bufs[(j + 1) % 2], s_bufs[j % 2]
            m_run, acc = None, None
            for c in range(n_chunks):
                if c == 0 and j >= 1:
                    finish(reduce_chunk(s_w, m_prev, n_chunks - 1, acc_prev), j - 1)
                m_run = score_chunk(q, kk_ref, s_w, c, m_run)
                if c >= 1:
                    acc = reduce_chunk(s_r, m_cur, c - 1, acc)
            m_prev, m_cur, acc_prev = m_cur, col_max(m_run), acc
        if j_hi == n_tiles:
            finish(reduce_chunk(s_bufs[(j_hi - 1) % 2], m_prev, n_chunks - 1, acc_prev), j_hi - 1)
        else:
            m_bufs[(j_hi - 1) % 2][...] = m_prev
            acc_ref[...] = acc_prev
        m_bufs[j_hi % 2][...] = m_cur

    def first_half(_, carry):
        run_tiles(0, n_tiles // 2)
        return carry

    lax.fori_loop(0, 1 + (pl.program_id(0) >> 30), first_half, 0)
    run_tiles(n_tiles // 2, n_tiles)


def _kv_head(h):
    return jnp.where(h < MLA_HEADS, h, MLA_HEADS + (h - MLA_HEADS) // GQA_GROUP)


def _attention(qT, k, vT):
    bsz, _, n_tiles, _, tq = qT.shape
    seq = n_tiles * tq
    assert n_tiles % 2 == 0
    n_steps = bsz * N_Q_HEADS

    def cur(g):
        return g // N_Q_HEADS, g % N_Q_HEADS

    def nxt(g):
        return cur(jnp.minimum(g + 1, n_steps - 1))

    def q_map(bh):
        return lambda g: (*bh(g), 0, 0, 0)

    def kv_map(bh):
        def index(g):
            b, h = bh(g)
            return b, _kv_head(h), 0, 0
        return index

    return pl.pallas_call(
        _attn_kernel,
        grid=(n_steps,),
        in_specs=[pl.BlockSpec((1, 1, n_tiles, HEAD_PAD, tq), q_map(cur)),
                  pl.BlockSpec((1, 1, 1, HEAD_PAD, tq), q_map(nxt)),
                  pl.BlockSpec((1, 1, seq, HEAD_PAD), kv_map(cur)),
                  pl.BlockSpec((1, 1, seq, HEAD_PAD), kv_map(nxt)),
                  pl.BlockSpec((1, 1, V_ROWS, seq), kv_map(cur))],
        out_specs=pl.BlockSpec((1, 1, n_tiles, V_DIM, tq), q_map(cur)),
        out_shape=jax.ShapeDtypeStruct((bsz, N_Q_HEADS, n_tiles, V_DIM, tq), _bf16),
        scratch_shapes=[pltpu.VMEM((seq, tq), _f32), pltpu.VMEM((seq, tq), _f32),
                        pltpu.VMEM((1, tq), _f32), pltpu.VMEM((1, tq), _f32),
                        pltpu.VMEM((V_ROWS, tq), _f32)],
        compiler_params=pltpu.CompilerParams(dimension_semantics=("arbitrary",),
                                             vmem_limit_bytes=VMEM_LIMIT_BYTES),
        name="attn",
    )(qT, qT, k, k, vT)


def _post_kernel(x_ref, o_ref_in, mod_ref, gom_ref, gog_ref, wout_ref, g2_ref, wgu_ref, wd_ref, gf_ref, out_ref,
                 *, final_norm):
    tm = x_ref.shape[0]
    x = x_ref[...]
    oT = jnp.concatenate([o_ref_in[0, :, t].reshape(N_Q_HEADS * V_DIM, TQ) for t in range(tm // TQ)],
                         axis=1).astype(_f32)
    half = MLA_HEADS * MLA_V_DIM
    on = jnp.concatenate([_rms_fm(oT[:half], gom_ref[...]), _rms_fm(oT[half:], gog_ref[...])], axis=0)
    y = _dot_tn(on.astype(_bf16), wout_ref[...])
    x = x + mod_ref[0, 5:6, :] * y
    h = _norm_mod(x, g2_ref[...], mod_ref[0, 6:7, :], mod_ref[0, 7:8, :]).astype(_bf16)
    x = x + (0.5 * mod_ref[0, 8:9, :]) * _swiglu(h, wgu_ref, wd_ref)
    if final_norm:
        r = lax.rsqrt(jnp.mean(x * x, axis=-1, keepdims=True) + EPS)
        x = x * r * gf_ref[...]
    out_ref[...] = x


def _post(x2d, oT, mod3, gom, gog, wout, g2, wgu, wd, gf, seq, final_norm):
    n, d = x2d.shape
    tm = TM_POST
    tpb = seq // tm
    return pl.pallas_call(
        functools.partial(_post_kernel, final_norm=final_norm),
        grid=(n // tm,),
        in_specs=[pl.BlockSpec((tm, d), lambda i: (i, 0)),
                  pl.BlockSpec((1, N_Q_HEADS, tm // TQ, V_DIM, TQ), lambda i: (i // tpb, 0, i % tpb, 0, 0)),
                  pl.BlockSpec((1, N_MOD, d), lambda i: (i // tpb, 0, 0)),
                  _const_spec(gom.shape),
                  _const_spec(gog.shape),
                  _const_spec(wout.shape),
                  _const_spec((1, d)),
                  _const_spec(wgu.shape),
                  _const_spec(wd.shape),
                  _const_spec((1, d))],
        out_specs=pl.BlockSpec((tm, d), lambda i: (i, 0)),
        out_shape=jax.ShapeDtypeStruct((n, d), _f32),
        compiler_params=pltpu.CompilerParams(vmem_limit_bytes=VMEM_LIMIT_BYTES),
        name="post",
    )(x2d, oT, mod3, gom, gog, wout, g2, wgu, wd, gf)


def _rope_table(seq, dim):
    rows = seq // GRID_W
    row = jnp.repeat(jnp.arange(rows), GRID_W).astype(_f32)
    col = jnp.tile(jnp.arange(GRID_W), rows).astype(_f32)
    axis_dim = dim // 2
    inv_freq = ROPE_THETA ** (-(jnp.arange(axis_dim // 2, dtype=_f32) * 2.0 / axis_dim))
    ang_row = row[:, None] * inv_freq[None, :]
    ang_col = col[:, None] * inv_freq[None, :]
    return jnp.concatenate([jnp.cos(ang_row), jnp.sin(ang_row), jnp.cos(ang_col), jnp.sin(ang_col)], axis=1).T


def kernel(x, c, w_ada, b_ada, g_ffn1, w1_gu, w1_down, g_mix, w_in, g_q_lat, w_uq, g_kv_lat, w_ukv,
           g_qhead, g_khead, g_out_mla, g_out_gqa, w_out, g_ffn2, w2_gu, w2_down, g_final):
    bsz, seq, d = x.shape
    depth = w_ada.shape[0]
    t64 = _rope_table(seq, GQA_HEAD_DIM)
    t32 = _rope_table(seq, MLA_ROPE_DIM)
    x2d = x.reshape(bsz * seq, d)
    for l in range(depth):
        mod3 = _ada(c, w_ada[l], b_ada[l]).reshape(bsz, N_MOD, d)

        x2d, w2_gu_bf, w2_down_bf = _ffn(x2d, mod3, g_ffn1[l].reshape(1, d), w1_gu[l].astype(_bf16),
                                         w1_down[l].astype(_bf16), seq, 0, w2_gu[l], w2_down[l])

        winT = w_in[l].T.astype(_bf16)
        wuq = w_uq[l].reshape(Q_LORA_RANK, MLA_HEADS, MLA_NOPE_DIM + MLA_ROPE_DIM)
        wuqT = jnp.concatenate([wuq[:, :, :MLA_NOPE_DIM].reshape(Q_LORA_RANK, -1),
                                wuq[:, :, MLA_NOPE_DIM:].reshape(Q_LORA_RANK, -1)], axis=1).T.astype(_bf16)
        wukv = w_ukv[l].reshape(KV_LORA_RANK, MLA_HEADS, MLA_NOPE_DIM + MLA_V_DIM)
        wukvT = jnp.concatenate([wukv[:, :, :MLA_NOPE_DIM].reshape(KV_LORA_RANK, -1),
                                 wukv[:, :, MLA_NOPE_DIM:].reshape(KV_LORA_RANK, -1)], axis=1).T.astype(_bf16)
        qT, k, vT = _mix_pre(x2d, mod3, g_mix[l].reshape(1, d), winT,
                             g_q_lat[l].reshape(-1, 1), wuqT, g_kv_lat[l].reshape(-1, 1), wukvT,
                             g_qhead[l].reshape(-1, 1), g_khead[l].reshape(-1, 1), t64, t32, bsz, seq)

        oT = _attention(qT, k, vT)

        x2d = _post(x2d, oT, mod3, g_out_mla[l].reshape(-1, 1), g_out_gqa[l].reshape(-1, 1),
                    w_out[l].astype(_bf16), g_ffn2[l].reshape(1, d),
                    w2_gu_bf, w2_down_bf, g_final.reshape(1, d), seq,
                    final_norm=(l == depth - 1))
    return x2d.reshape(bsz, seq, d)
```
